```python
import jax
import jax.numpy as jnp
from jax import lax
import numpy as np

D_MODEL = 1024
BATCH = 8
SEQ = 4096
DEPTH = 2

GRID_W = 64
CTX_LEN = 256
N_MIXERS = 2
N_LAYERS_A = (DEPTH + 1) // 2
N_LAYERS_B = DEPTH // 2
EPS = 1e-6
HG_HEADS = 8
HG_DK = D_MODEL // HG_HEADS
HG_DV = D_MODEL // HG_HEADS
HG_CHUNK = 32
CONV_WIDTH = 31
PEER_HEADS = 8
PEER_NKEYS = 128
PEER_EXPERTS = PEER_NKEYS * PEER_NKEYS
PEER_DKEY = 256
PEER_TOPK = 16
PEER_BLOCK = 128

kernel_name = 'hybrid_hgrn2_conformer_peer_dit'


def rms_norm(x, g):
    xf = x.astype(jnp.float32)
    y = xf * lax.rsqrt(jnp.mean(xf * xf, axis=-1, keepdims=True) + EPS)
    return (y * g.astype(jnp.float32)).astype(x.dtype)


def layer_norm(x, g, b):
    xf = x.astype(jnp.float32)
    xc = xf - jnp.mean(xf, axis=-1, keepdims=True)
    y = xc * lax.rsqrt(jnp.mean(xc * xc, axis=-1, keepdims=True) + EPS)
    return (y * g.astype(jnp.float32) + b.astype(jnp.float32)).astype(x.dtype)


def modulate(h, shift, scale):
    return h * (1 + scale) + shift


def forget_gate(z, lb):
    B, L, _ = z.shape
    zf = z.astype(jnp.float32)
    log_f = jnp.logaddexp(jnp.log(lb), jnp.log1p(-lb) + jax.nn.log_sigmoid(zf))
    k = (1.0 - lb) * jax.nn.sigmoid(-zf)
    shp = (B, L, HG_HEADS, HG_DK)
    return log_f.reshape(shp), k.reshape(shp)


def gla_chunk_scan(q, k, v, log_f, s0, with_output):
    B, L, H, _ = k.shape
    n = L // HG_CHUNK

    def to_chunks(t):
        return jnp.moveaxis(t.astype(jnp.float32).reshape(B, n, HG_CHUNK, H, t.shape[-1]), 1, 0)

    k_c, v_c = to_chunks(k), to_chunks(v)
    b = jnp.cumsum(to_chunks(log_f), axis=2)
    b_end = b[:, :, -1]
    k_end = k_c * jnp.exp(b_end[:, :, None] - b)
    lower = jnp.tril(jnp.ones((HG_CHUNK, HG_CHUNK), dtype=bool))[None, :, :, None, None]

    def step(S, xs):
        ke, vc, be = xs[:3]
        S_next = jnp.exp(be)[..., None] * S + jnp.einsum('bchk,bchv->bhkv', ke, vc)
        if not with_output:
            return S_next, None
        bc, qc, kc = xs[3:]
        inter = jnp.einsum('bchk,bhkv->bchv', qc * jnp.exp(bc), S)
        decay = jnp.exp(jnp.where(lower, bc[:, :, None] - bc[:, None], -jnp.inf))
        scores = jnp.einsum('bthk,bshk,btshk->bths', qc, kc, decay)
        intra = jnp.einsum('bths,bshv->bthv', scores, vc)
        return S_next, inter + intra

    xs = (k_end, v_c, b_end)
    if with_output:
        xs = xs + (b, to_chunks(q), k_c)
    S_final, out = lax.scan(step, s0, xs)
    if not with_output:
        return S_final, None
    return S_final, jnp.moveaxis(out, 0, 1).reshape(B, L, H, HG_DV)


def hgrn2_mixer(h, w_in, w_out, gnorm_g, lb, s0_f, s0_b, with_output):
    B, L, D = h.shape
    n_cols = 5 * D if with_output else 3 * D
    z = h @ w_in[:, :n_cols]
    v = z[..., :D].reshape(B, L, HG_HEADS, HG_DV)
    log_f_f, k_f = forget_gate(z[..., D:2 * D], lb[:D])
    log_f_b, k_b = forget_gate(z[..., 2 * D:3 * D], lb[D:])
    q = z[..., 3 * D:4 * D].reshape(B, L, HG_HEADS, HG_DK) if with_output else None
    rev = lambda t: None if t is None else jnp.flip(t, axis=1)
    s_f, o_f = gla_chunk_scan(q, k_f, v, log_f_f, s0_f, with_output)
    s_b, o_b = gla_chunk_scan(rev(q), rev(k_b), rev(v), rev(log_f_b), s0_b, with_output)
    if not with_output:
        return None, s_f, s_b
    o = rms_norm(o_f + rev(o_b), gnorm_g)
    gate = z[..., 4 * D:].reshape(B, L, HG_HEADS, HG_DV)
    o = (o * jax.nn.silu(gate.astype(jnp.float32))).reshape(B, L, D).astype(h.dtype)
    return o @ w_out, s_f, s_b


def dwconv(y, w):
    pad = (CONV_WIDTH - 1) // 2
    return lax.conv_general_dilated(
        y, w[:, None, :].astype(y.dtype), window_strides=(1,), padding=[(pad, pad)],
        dimension_numbers=('NWC', 'WIO', 'NWC'), feature_group_count=y.shape[-1])


def axial_dwconv(y, w):
    B, L, C = y.shape
    rows = L // GRID_W
    half = C // 2
    g = y.reshape(B, rows, GRID_W, C)
    along_w = dwconv(g[..., :half].reshape(B * rows, GRID_W, half), w[:, :half])
    along_w = along_w.reshape(B, rows, GRID_W, half)
    gc = jnp.swapaxes(g[..., half:], 1, 2).reshape(B * GRID_W, rows, C - half)
    along_h = jnp.swapaxes(dwconv(gc, w[:, half:]).reshape(B, GRID_W, rows, C - half), 1, 2)
    return jnp.concatenate([along_w, along_h], axis=-1).reshape(B, L, C)


def conformer_conv(h, w_pw1, b_pw1, w_dw, b_dw, ln_g, ln_b, w_pw2, b_pw2, on_grid):
    a, g = jnp.split(h @ w_pw1 + b_pw1, 2, axis=-1)
    y = a * jax.nn.sigmoid(g)
    y = axial_dwconv(y, w_dw) if on_grid else dwconv(y, w_dw)
    y = jax.nn.silu(layer_norm(y + b_dw, ln_g, ln_b))
    return y @ w_pw2 + b_pw2


def peer(h, w_q, sub_keys, u, v):
    B, L, D = h.shape
    hb = h.reshape(B * L // PEER_BLOCK, PEER_BLOCK, D)

    def block(ht):
        t = ht.shape[0]
        q = (ht @ w_q).reshape(t, PEER_HEADS, 2, PEER_DKEY // 2)
        s = jnp.einsum('thpd,hpnd->thpn', q, sub_keys).astype(jnp.float32)
        s_val, s_idx = lax.top_k(s, PEER_TOPK)
        cand = (s_val[:, :, 0, :, None] + s_val[:, :, 1, None, :]).reshape(
            t, PEER_HEADS, PEER_TOPK * PEER_TOPK)
        c_val, c_idx = lax.top_k(cand, PEER_TOPK)
        i1 = jnp.take_along_axis(s_idx[:, :, 0], c_idx // PEER_TOPK, axis=-1)
        i2 = jnp.take_along_axis(s_idx[:, :, 1], c_idx % PEER_TOPK, axis=-1)
        eid = (i1 * PEER_NKEYS + i2).reshape(t, PEER_HEADS * PEER_TOPK)
        gate = jax.nn.softmax(c_val, axis=-1).reshape(t, PEER_HEADS * PEER_TOPK)
        u_sel = jnp.take(u, eid, axis=0)
        v_sel = jnp.take(v, eid, axis=0)
        act = jax.nn.gelu(jnp.einsum('td,ted->te', ht, u_sel).astype(jnp.float32), approximate=False)
        return jnp.einsum('te,ted->td', (gate * act).astype(ht.dtype), v_sel)

    return lax.map(block, hb).reshape(B, L, D)


def setup_inputs(seed: int = 0) -> dict:
    key = jax.random.key(seed)
    k = jax.random.split(key, 25)
    D = D_MODEL

    def nrm(i, shape, scale):
        return jax.random.normal(k[i], shape, jnp.float32) * scale

    return {
        'x': nrm(0, (BATCH, SEQ, D), 1.0),
        'c': nrm(1, (BATCH, D), 1.0),
        'ctx': nrm(2, (BATCH, CTX_LEN, D), 1.0),
        'c_ctx': nrm(3, (D,), 1.0),
        'w_ada': nrm(4, (DEPTH, D, 6 * D), 0.5 * D ** -0.5),
        'b_ada': nrm(5, (DEPTH, 6 * D), 0.02),
        'g_mix': 1.0 + nrm(6, (DEPTH, D), 0.02),
        'g_ffn': 1.0 + nrm(7, (DEPTH, D), 0.02),
        'hg_w_in': nrm(8, (N_LAYERS_A, D, 5 * D), D ** -0.5),
        'hg_w_out': nrm(9, (N_LAYERS_A, D, D), D ** -0.5),
        'hg_gnorm': 1.0 + nrm(10, (N_LAYERS_A, HG_DV), 0.02),
        'hg_lb': nrm(11, (DEPTH + 1, 2 * D), 0.1),
        'cv_w_pw1': nrm(12, (N_LAYERS_B, D, 2 * D), D ** -0.5),
        'cv_b_pw1': nrm(13, (N_LAYERS_B, 2 * D), 0.02),
        'cv_w_dw': nrm(14, (N_LAYERS_B, CONV_WIDTH, D), CONV_WIDTH ** -0.5),
        'cv_b_dw': nrm(15, (N_LAYERS_B, D), 0.02),
        'cv_ln_g': 1.0 + nrm(16, (N_LAYERS_B, D), 0.02),
        'cv_ln_b': nrm(17, (N_LAYERS_B, D), 0.02),
        'cv_w_pw2': nrm(18, (N_LAYERS_B, D, D), D ** -0.5),
        'cv_b_pw2': nrm(19, (N_LAYERS_B, D), 0.02),
        'peer_w_q': nrm(20, (DEPTH, D, PEER_HEADS * PEER_DKEY), D ** -0.5),
        'peer_keys': nrm(21, (DEPTH, PEER_HEADS, 2, PEER_NKEYS, PEER_DKEY // 2), (PEER_DKEY // 2) ** -0.5),
        'peer_u': nrm(22, (DEPTH, PEER_EXPERTS, D), D ** -0.5),
        'peer_v': nrm(23, (DEPTH, PEER_EXPERTS, D), 1.0),
        'g_final': 1.0 + nrm(24, (D,), 0.02),
    }


def reference(x, c, ctx, c_ctx, w_ada, b_ada, g_mix, g_ffn, hg_w_in, hg_w_out, hg_gnorm, hg_lb,
              cv_w_pw1, cv_b_pw1, cv_w_dw, cv_b_dw, cv_ln_g, cv_ln_b, cv_w_pw2, cv_b_pw2,
              peer_w_q, peer_keys, peer_u, peer_v, g_final):
    B = x.shape[0]
    lb_all = jnp.cumsum(jax.nn.softmax(hg_lb.astype(jnp.float32), axis=0), axis=0)
    s_c = jax.nn.silu(c)
    s_ctx = jax.nn.silu(c_ctx)
    for l in range(DEPTH):
        mixer = l % N_MIXERS
        ctx_later = any(j % N_MIXERS == 0 for j in range(l + 1, DEPTH))
        ctx_here = mixer == 0 or ctx_later
        sh_m, sc_m, gt_m, sh_f, sc_f, gt_f = jnp.split(
            (s_c @ w_ada[l] + b_ada[l])[:, None, :], 6, axis=-1)
        hx = modulate(rms_norm(x, g_mix[l]), sh_m, sc_m)
        if ctx_here:
            csh_m, csc_m, cgt_m, csh_f, csc_f, cgt_f = jnp.split(s_ctx @ w_ada[l] + b_ada[l], 6)
            hc = modulate(rms_norm(ctx, g_mix[l]), csh_m, csc_m)
        if mixer == 0:
            a = l // N_MIXERS
            s0 = jnp.zeros((B, HG_HEADS, HG_DK, HG_DV), jnp.float32)
            ctx_mix, s_f, s_b = hgrn2_mixer(hc, hg_w_in[a], hg_w_out[a], hg_gnorm[a], lb_all[l],
                                            s0, s0, ctx_later)
            x_mix, _, _ = hgrn2_mixer(hx, hg_w_in[a], hg_w_out[a], hg_gnorm[a], lb_all[l],
                                      s_f, s_b, True)
        else:
            bi = l // N_MIXERS
            conv_w = (cv_w_pw1[bi], cv_b_pw1[bi], cv_w_dw[bi], cv_b_dw[bi], cv_ln_g[bi],
                      cv_ln_b[bi], cv_w_pw2[bi], cv_b_pw2[bi])
            x_mix = conformer_conv(hx, *conv_w, on_grid=True)
            if ctx_later:
                ctx_mix = conformer_conv(hc, *conv_w, on_grid=False)
        peer_w = (peer_w_q[l], peer_keys[l], peer_u[l], peer_v[l])
        x = x + gt_m * x_mix.astype(x.dtype)
        x = x + gt_f * peer(modulate(rms_norm(x, g_ffn[l]), sh_f, sc_f), *peer_w)
        if ctx_later:
            ctx = ctx + cgt_m * ctx_mix.astype(ctx.dtype)
            ctx = ctx + cgt_f * peer(modulate(rms_norm(ctx, g_ffn[l]), csh_f, csc_f), *peer_w)
    return rms_norm(x, g_final)
```

```python
import functools
import math

import numpy as np
import jax
import jax.numpy as jnp
from jax import lax
from jax.experimental import pallas as pl
from jax.experimental.pallas import tpu as pltpu

EPS = 1e-6
GRID_W = 64
CONV_WIDTH = 31
HG_HEADS = 8
PEER_HEADS = 8
PEER_NKEYS = 128
PEER_DKEY = 256
PEER_TOPK = 16
N_MIXERS = 2

LANES = 128
SUBLANES = 8
SCAN_CHUNK = 128
VMEM_LIMIT = 56 * 1024 * 1024

F32 = jnp.float32
BF16 = jnp.bfloat16


def _tile(n, pref):
    t = min(n, pref)
    while n % t:
        t //= 2
    return t


def _params(sem):
    return pltpu.CompilerParams(dimension_semantics=sem, vmem_limit_bytes=VMEM_LIMIT)


def _rmsnorm_mod(x, g, sh, sc):
    y = x * lax.rsqrt(jnp.mean(x * x, axis=-1, keepdims=True) + EPS) * g
    return y * (1.0 + sc) + sh


def _ada_kernel(c_ref, w_ref, b_ref, o_ref):
    s = c_ref[...]
    s = s * jax.nn.sigmoid(s)
    o_ref[...] = jnp.dot(s, w_ref[...], preferred_element_type=F32,
                         precision=lax.Precision.HIGHEST) + b_ref[...]


def _ada_call(cc, w_ada, b_ada):
    depth, d, n = w_ada.shape
    r = cc.shape[0]
    tn = _tile(n, 1024)
    return pl.pallas_call(
        _ada_kernel,
        grid=(depth, n // tn),
        in_specs=[pl.BlockSpec((r, d), lambda l, j: (0, 0)),
                  pl.BlockSpec((None, d, tn), lambda l, j: (l, 0, j)),
                  pl.BlockSpec((None, 1, tn), lambda l, j: (l, 0, j))],
        out_specs=pl.BlockSpec((None, r, tn), lambda l, j: (l, 0, j)),
        out_shape=jax.ShapeDtypeStruct((depth, r, n), F32),
        compiler_params=_params(("parallel", "parallel")),
        name="ada",
    )(cc, w_ada, b_ada.reshape(depth, 1, n))


def _nmm_kernel(x_ref, g_ref, sh_ref, sc_ref, w_ref, o_ref, h_scr):
    @pl.when(pl.program_id(1) == 0)
    def _():
        h_scr[...] = _rmsnorm_mod(x_ref[...], g_ref[...], sh_ref[...], sc_ref[...]).astype(BF16)

    o_ref[...] = jnp.dot(h_scr[...], w_ref[...], preferred_element_type=F32).astype(o_ref.dtype)


def _nmm_glu_kernel(x_ref, g_ref, sh_ref, sc_ref, wa_ref, wg_ref, ba_ref, bg_ref, o_ref, h_scr):
    @pl.when(pl.program_id(1) == 0)
    def _():
        h_scr[...] = _rmsnorm_mod(x_ref[...], g_ref[...], sh_ref[...], sc_ref[...]).astype(BF16)

    h = h_scr[...]
    a = jnp.dot(h, wa_ref[...], preferred_element_type=F32) + ba_ref[...]
    g = jnp.dot(h, wg_ref[...], preferred_element_type=F32) + bg_ref[...]
    o_ref[...] = (a * jax.nn.sigmoid(g)).astype(o_ref.dtype)


def _nmm_call(x2, g, sh, sc, w, rows_per_mod, out_dtype, glu_bias=None):
    m, d = x2.shape
    n = w.shape[1]
    tm = _tile(rows_per_mod, 1024)
    per = rows_per_mod // tm
    x_spec = pl.BlockSpec((tm, d), lambda i, j: (i, 0))
    g_spec = pl.BlockSpec((1, d), lambda i, j: (0, 0))
    mod_spec = pl.BlockSpec((None, 1, d), lambda i, j: (i // per, 0, 0))
    scratch = [pltpu.VMEM((tm, d), BF16)]
    if glu_bias is None:
        tn = _tile(n, 1024)
        return pl.pallas_call(
            _nmm_kernel,
            grid=(m // tm, n // tn),
            in_specs=[x_spec, g_spec, mod_spec, mod_spec,
                      pl.BlockSpec((d, tn), lambda i, j: (0, j))],
            out_specs=pl.BlockSpec((tm, tn), lambda i, j: (i, j)),
            out_shape=jax.ShapeDtypeStruct((m, n), out_dtype),
            scratch_shapes=scratch,
            compiler_params=_params(("parallel", "arbitrary")),
            name="norm_mod_matmul",
        )(x2, g, sh, sc, w)
    nh = n // 2
    tn = _tile(nh, 512)
    nb = nh // tn
    return pl.pallas_call(
        _nmm_glu_kernel,
        grid=(m // tm, nb),
        in_specs=[x_spec, g_spec, mod_spec, mod_spec,
                  pl.BlockSpec((d, tn), lambda i, j: (0, j)),
                  pl.BlockSpec((d, tn), lambda i, j: (0, j + nb)),
                  pl.BlockSpec((1, tn), lambda i, j: (0, j)),
                  pl.BlockSpec((1, tn), lambda i, j: (0, j + nb))],
        out_specs=pl.BlockSpec((tm, tn), lambda i, j: (i, j)),
        out_shape=jax.ShapeDtypeStruct((m, nh), out_dtype),
        scratch_shapes=scratch,
        compiler_params=_params(("parallel", "arbitrary")),
        name="norm_mod_matmul_glu",
    )(x2, g, sh, sc, w, w, glu_bias, glu_bias)


@functools.lru_cache(maxsize=None)
def _scan_consts(c):
    nl = int(math.log2(c))
    assert 1 << nl == c
    idx = np.arange(c)
    mats = np.zeros((2, 2 + 2 * nl, c, c), np.float32)
    level = np.zeros((2, c, c), np.int32)
    for d in range(2):
        tau = idx if d == 0 else c - 1 - idx
        tt, tr = tau[:, None], tau[None, :]
        mats[d, 0] = tr <= tt
        mats[d, 1] = tr > tt
        for l in range(nl):
            half = 1 << l
            mid = (tau // (2 * half)) * (2 * half) + half - 1
            later = ((tau >> l) & 1) == 1
            mats[d, 2 + 2 * l] = later[:, None] & (tr > mid[:, None]) & (tr <= tt)
            mats[d, 3 + 2 * l] = (~later)[:, None] & (tr > tt) & (tr <= mid[:, None])
        xor = tt ^ tr
        lev = np.floor(np.log2(np.maximum(xor, 1))).astype(np.int32)
        level[d] = np.where(tt > tr, lev, np.where(tt == tr, -1, -2))
    return mats.reshape(2, (2 + 2 * nl) * c, c), level, nl


def _scan_direction(d, zf, v, q, lb, est_ref, lev_ref, st_ref, o_ref, n_levels):
    c = zf.shape[0]
    heads = st_ref.shape[0]
    f = lb + (1.0 - lb) * jax.nn.sigmoid(zf)
    logf = jnp.log(f)
    kk = (1.0 - lb) * jax.nn.sigmoid(-zf)
    lev = lev_ref[d]
    for h in range(heads):
        sl = slice(h * LANES, (h + 1) * LANES)
        lf = logf[:, sl]
        hi = lf.astype(BF16)
        mid = (lf - hi.astype(F32)).astype(BF16)
        r = jnp.dot(est_ref[d], jnp.concatenate([hi, mid], axis=1), preferred_element_type=F32)
        ex = r[:, :LANES] + r[:, LANES:]
        p = jnp.exp(ex)
        dec = jnp.exp(ex[0:1, :] + ex[c:c + 1, :])
        q_h, k_h, v_h = q[:, sl], kk[:, sl], v[:, sl]
        st = st_ref[h]
        qb = (q_h * p[0:c]).astype(BF16)
        inter = lax.dot_general(qb, st.astype(BF16), (((1,), (1,)), ((), ())),
                                preferred_element_type=F32)
        pr = lax.dot_general(q_h.astype(BF16), k_h.astype(BF16), (((1,), (1,)), ((), ())),
                             preferred_element_type=F32)
        scores = jnp.where(lev == -1, pr, 0.0)
        for l in range(n_levels):
            ql = (q_h * p[(2 + 2 * l) * c:(3 + 2 * l) * c]).astype(BF16)
            kl = (k_h * p[(3 + 2 * l) * c:(4 + 2 * l) * c]).astype(BF16)
            pr = lax.dot_general(ql, kl, (((1,), (1,)), ((), ())), preferred_element_type=F32)
            scores = jnp.where(lev == l, pr, scores)
        intra = jnp.dot(scores.astype(BF16), v_h.astype(BF16), preferred_element_type=F32)
        o_ref[:, sl] = inter + intra
        kend = (k_h * p[c:2 * c]).astype(BF16)
        st_ref[h] = st * dec + jnp.dot(v_h.T.astype(BF16), kend, preferred_element_type=F32)


def _scan_kernel(lbraw_ref, est_ref, lev_ref, s0f_ref, s0b_ref,
                 vf_ref, ff_ref, qf_ref, vb_ref, fb_ref, qb_ref,
                 of_ref, ob_ref, sf_ref, sb_ref, stf_scr, stb_scr, *, layer, n_levels):
    j = pl.program_id(1)

    @pl.when(j == 0)
    def _():
        stf_scr[...] = s0f_ref[...]
        stb_scr[...] = s0b_ref[...]

    raw = lbraw_ref[...]
    e = jnp.exp(raw - jnp.max(raw, axis=0, keepdims=True))
    lb_all = jnp.sum(e[0:layer + 1], axis=0, keepdims=True) / jnp.sum(e, axis=0, keepdims=True)
    d_model = vf_ref.shape[-1]
    _scan_direction(0, ff_ref[...], vf_ref[...], qf_ref[...], lb_all[:, :d_model],
                    est_ref, lev_ref, stf_scr, of_ref, n_levels)
    _scan_direction(1, fb_ref[...], vb_ref[...], qb_ref[...], lb_all[:, d_model:],
                    est_ref, lev_ref, stb_scr, ob_ref, n_levels)

    @pl.when(j == pl.num_programs(1) - 1)
    def _():
        sf_ref[...] = stf_scr[...]
        sb_ref[...] = stb_scr[...]


def _scan_call(z, hg_lb, s0f, s0b, layer):
    b, l, d5 = z.shape
    d = d5 // 5
    heads = d // LANES
    c = SCAN_CHUNK
    nc = l // c
    est, lev, n_levels = _scan_consts(c)
    est = jnp.asarray(est, BF16)
    lev = jnp.asarray(lev)

    def zspec(col, rev):
        if rev:
            return pl.BlockSpec((None, c, d), lambda bi, j: (bi, nc - 1 - j, col))
        return pl.BlockSpec((None, c, d), lambda bi, j: (bi, j, col))

    full = lambda a: pl.BlockSpec(a.shape, lambda bi, j: (0,) * a.ndim)
    st_spec = pl.BlockSpec((None, heads, LANES, LANES), lambda bi, j: (bi, 0, 0, 0))
    return pl.pallas_call(
        functools.partial(_scan_kernel, layer=layer, n_levels=n_levels),
        grid=(b, nc),
        in_specs=[full(hg_lb), full(est), full(lev), st_spec, st_spec,
                  zspec(0, False), zspec(1, False), zspec(3, False),
                  zspec(0, True), zspec(2, True), zspec(3, True)],
        out_specs=[pl.BlockSpec((None, c, d), lambda bi, j: (bi, j, 0)),
                   pl.BlockSpec((None, c, d), lambda bi, j: (bi, nc - 1 - j, 0)),
                   st_spec, st_spec],
        out_shape=[jax.ShapeDtypeStruct((b, l, d), F32), jax.ShapeDtypeStruct((b, l, d), F32),
                   jax.ShapeDtypeStruct(s0f.shape, F32), jax.ShapeDtypeStruct(s0b.shape, F32)],
        scratch_shapes=[pltpu.VMEM((heads, LANES, LANES), F32), pltpu.VMEM((heads, LANES, LANES), F32)],
        compiler_params=_params(("parallel", "arbitrary")),
        name="hgrn2_scan",
    )(hg_lb, est, lev, s0f, s0b, z, z, z, z, z, z)


def _mix_out_kernel(of_ref, ob_ref, gate_ref, gn_ref, w_ref, x_ref, gt_ref, o_ref, h_scr):
    @pl.when(pl.program_id(1) == 0)
    def _():
        o = of_ref[...] + ob_ref[...]
        gate = gate_ref[...]
        gn = gn_ref[...]
        for h in range(o.shape[1] // LANES):
            sl = slice(h * LANES, (h + 1) * LANES)
            oh = o[:, sl]
            y = oh * lax.rsqrt(jnp.mean(oh * oh, axis=-1, keepdims=True) + EPS) * gn
            gh = gate[:, sl]
            h_scr[:, sl] = (y * (gh * jax.nn.sigmoid(gh))).astype(BF16)

    mix = jnp.dot(h_scr[...], w_ref[...], preferred_element_type=F32)
    o_ref[...] = x_ref[...] + gt_ref[...] * mix


def _mix_out_call(o_f, o_b, z2, gnorm, w_out, x2, gt, rows_per_mod):
    m, d = x2.shape
    tm = _tile(rows_per_mod, 1024)
    per = rows_per_mod // tm
    tn = _tile(d, 512)
    row = pl.BlockSpec((tm, d), lambda i, j: (i, 0))
    return pl.pallas_call(
        _mix_out_kernel,
        grid=(m // tm, d // tn),
        in_specs=[row, row,
                  pl.BlockSpec((tm, d), lambda i, j: (i, 4)),
                  pl.BlockSpec((1, LANES), lambda i, j: (0, 0)),
                  pl.BlockSpec((d, tn), lambda i, j: (0, j)),
                  pl.BlockSpec((tm, tn), lambda i, j: (i, j)),
                  pl.BlockSpec((None, 1, tn), lambda i, j: (i // per, 0, j))],
        out_specs=pl.BlockSpec((tm, tn), lambda i, j: (i, j)),
        out_shape=jax.ShapeDtypeStruct((m, d), F32),
        scratch_shapes=[pltpu.VMEM((tm, d), BF16)],
        compiler_params=_params(("parallel", "arbitrary")),
        name="hgrn2_out",
    )(o_f, o_b, z2, gnorm, w_out, x2, gt)


def _conv_kernel(y_ref, w_ref, o_ref, padw_scr, padh_scr, *, rows, n_w_tiles):
    ct = pl.program_id(1)
    half = (CONV_WIDTH - 1) // 2
    l, lanes = y_ref.shape
    lead = padw_scr.shape[1] - GRID_W - 16

    @pl.when(ct < n_w_tiles)
    def _():
        padw_scr[...] = jnp.zeros(padw_scr.shape, F32)
        padw_scr[:, lead:lead + GRID_W, :] = y_ref[...].reshape(rows, GRID_W, lanes)
        acc = jnp.zeros((rows, GRID_W, lanes), F32)
        for k in range(CONV_WIDTH):
            start = lead + k - half
            acc = acc + w_ref[k:k + 1, :].reshape(1, 1, lanes) * padw_scr[:, start:start + GRID_W, :]
        o_ref[...] = acc.reshape(l, lanes)

    @pl.when(ct >= n_w_tiles)
    def _():
        margin = half * GRID_W
        padh_scr[...] = jnp.zeros(padh_scr.shape, F32)
        padh_scr[margin:margin + l, :] = y_ref[...]
        acc = jnp.zeros((l, lanes), F32)
        for k in range(CONV_WIDTH):
            acc = acc + w_ref[k:k + 1, :] * padh_scr[k * GRID_W:k * GRID_W + l, :]
        o_ref[...] = acc


def _conv_call(y, w_dw):
    b, l, c = y.shape
    rows = l // GRID_W
    half = (CONV_WIDTH - 1) // 2
    n_w_tiles = (c // 2) // LANES
    wp = jnp.pad(w_dw, ((0, 32 - CONV_WIDTH), (0, 0)))
    return pl.pallas_call(
        functools.partial(_conv_kernel, rows=rows, n_w_tiles=n_w_tiles),
        grid=(b, c // LANES),
        in_specs=[pl.BlockSpec((None, l, LANES), lambda bi, ct: (bi, 0, ct)),
                  pl.BlockSpec((32, LANES), lambda bi, ct: (0, ct))],
        out_specs=pl.BlockSpec((None, l, LANES), lambda bi, ct: (bi, 0, ct)),
        out_shape=jax.ShapeDtypeStruct((b, l, c), F32),
        scratch_shapes=[pltpu.VMEM((rows, GRID_W + 32, LANES), F32),
                        pltpu.VMEM((l + 2 * half * GRID_W, LANES), F32)],
        compiler_params=_params(("parallel", "parallel")),
        name="axial_dwconv",
    )(y, wp)


def _ln_out_kernel(y_ref, bdw_ref, lg_ref, lb_ref, w_ref, b2_ref, x_ref, gt_ref, o_ref, h_scr):
    @pl.when(pl.program_id(1) == 0)
    def _():
        y = y_ref[...] + bdw_ref[...]
        yc = y - jnp.mean(y, axis=-1, keepdims=True)
        yn = yc * lax.rsqrt(jnp.mean(yc * yc, axis=-1, keepdims=True) + EPS)
        yn = yn * lg_ref[...] + lb_ref[...]
        h_scr[...] = (yn * jax.nn.sigmoid(yn)).astype(BF16)

    mix = jnp.dot(h_scr[...], w_ref[...], preferred_element_type=F32) + b2_ref[...]
    o_ref[...] = x_ref[...] + gt_ref[...] * mix


def _ln_out_call(y2, b_dw, ln_g, ln_b, w_pw2, b_pw2, x2, gt, rows_per_mod):
    m, d = x2.shape
    tm = _tile(rows_per_mod, 1024)
    per = rows_per_mod // tm
    tn = _tile(d, 512)
    vec = pl.BlockSpec((1, d), lambda i, j: (0, 0))
    return pl.pallas_call(
        _ln_out_kernel,
        grid=(m // tm, d // tn),
        in_specs=[pl.BlockSpec((tm, d), lambda i, j: (i, 0)), vec, vec, vec,
                  pl.BlockSpec((d, tn), lambda i, j: (0, j)),
                  pl.BlockSpec((1, tn), lambda i, j: (0, j)),
                  pl.BlockSpec((tm, tn), lambda i, j: (i, j)),
                  pl.BlockSpec((None, 1, tn), lambda i, j: (i // per, 0, j))],
        out_specs=pl.BlockSpec((tm, tn), lambda i, j: (i, j)),
        out_shape=jax.ShapeDtypeStruct((m, d), F32),
        scratch_shapes=[pltpu.VMEM((tm, d), BF16)],
        compiler_params=_params(("parallel", "arbitrary")),
        name="conv_out",
    )(y2, b_dw, ln_g, ln_b, w_pw2, b_pw2, x2, gt)


def _vmax(a, b):
    if a is None:
        return b
    if b is None:
        return a
    return jnp.maximum(a, b)


def _vmin(a, b):
    if a is None or b is None:
        return None
    return jnp.minimum(a, b)


def _bitonic_merge_desc(xs):
    n = len(xs)
    j = n // 2
    while j >= 1:
        for i in range(n):
            o = i ^ j
            if o > i:
                a, b = xs[i], xs[o]
                xs[i], xs[o] = _vmax(a, b), _vmin(a, b)
        j //= 2
    return xs


def _bitonic_sort_desc(xs):
    n = len(xs)
    k = 2
    while k <= n:
        j = k // 2
        while j >= 1:
            for i in range(n):
                o = i ^ j
                if o > i:
                    a, b = xs[i], xs[o]
                    if (i & k) == 0:
                        xs[i], xs[o] = _vmax(a, b), _vmin(a, b)
                    else:
                        xs[i], xs[o] = _vmin(a, b), _vmax(a, b)
            j //= 2
        k *= 2
    return xs


def _merge_top(a, b):
    n = len(a)
    b = list(b) + [None] * (n - len(b))
    return _bitonic_merge_desc([_vmax(a[i], b[n - 1 - i]) for i in range(n)])


def _topk_kernel(q_ref, keys_ref, th_ref, e1_ref, s2_ref, e2_ref, s_scr):
    heads = th_ref.shape[0]
    nk = keys_ref.shape[1]
    k = PEER_TOPK
    t = q_ref.shape[0]
    sub = lax.broadcasted_iota(jnp.int32, (SUBLANES, t), 0)
    tops = [[None] * k, [None] * k]
    for h in range(heads):
        for p in range(2):
            hp = 2 * h + p
            dk = keys_ref.shape[2]
            s = lax.dot_general(keys_ref[hp], q_ref[:, hp * dk:(hp + 1) * dk],
                                (((1,), (1,)), ((), ())), preferred_element_type=F32)
            s_scr[hp] = s
            groups = _bitonic_sort_desc([s[SUBLANES * a:SUBLANES * (a + 1), :] for a in range(nk // SUBLANES)])
            assert len(groups) == k
            for shift in (4, 2, 1):
                groups = _merge_top(groups, [pltpu.roll(g, shift, 0) for g in groups])
            for a in range(k):
                prev = tops[p][a]
                tops[p][a] = groups[a] if prev is None else jnp.where(sub == h, groups[a], prev)
    v1, v2 = tops
    best = [v1[0] + v2[b] for b in range(k)]
    for a in range(1, k // 2):
        best = _merge_top(best, [v1[a] + v2[b] for b in range(k // (a + 1))])
    best = _merge_top(best, [v1[a] + v2[0] for a in range(k // 2, k)])
    cmax = best[0]
    zsum = jnp.ones_like(cmax)
    for a in range(1, k):
        zsum = zsum + jnp.exp(best[a] - cmax)
    tau = best[k - 1]
    inv_z = 1.0 / zsum
    for h in range(heads):
        s1 = s_scr[2 * h]
        s2 = s_scr[2 * h + 1]
        tau_h = tau[h:h + 1, :]
        slack = (jnp.abs(tau_h) + jnp.abs(s1)) * (2.0 ** -22)
        th_ref[h] = (tau_h - s1) - slack
        e1_ref[h] = jnp.exp(s1 - v1[0][h:h + 1, :])
        s2_ref[h] = s2
        e2_ref[h] = jnp.exp(s2 - v2[0][h:h + 1, :]) * inv_z[h:h + 1, :]


def _topk_call(qk, keys):
    m = qk.shape[0]
    hp, nk, dk = keys.shape
    heads = hp // 2
    t = _tile(m, 512)
    out = jax.ShapeDtypeStruct((heads, nk, m), F32)
    ospec = pl.BlockSpec((heads, nk, t), lambda i: (0, 0, i))
    return pl.pallas_call(
        _topk_kernel,
        grid=(m // t,),
        in_specs=[pl.BlockSpec((t, hp * dk), lambda i: (i, 0)),
                  pl.BlockSpec((hp, nk, dk), lambda i: (0, 0, 0))],
        out_specs=[ospec, ospec, ospec, ospec],
        out_shape=[out, out, out, out],
        scratch_shapes=[pltpu.VMEM((hp, nk, t), F32)],
        compiler_params=_params(("parallel",)),
        name="peer_topk",
    )(qk, keys)


def _peer_kernel(x_ref, g_ref, sh_ref, sc_ref, gt_ref, th_ref, e1_ref, s2_ref, e2_ref,
                 u_ref, vt_ref, o_ref, ht_scr, acc_scr):
    kb = pl.program_id(1)
    heads, nk, t = th_ref.shape
    eb = u_ref.shape[0]
    groups = eb // nk

    @pl.when(kb == 0)
    def _():
        h = _rmsnorm_mod(x_ref[...], g_ref[...], sh_ref[...], sc_ref[...])
        ht_scr[...] = h.T.astype(BF16)
        acc_scr[...] = jnp.zeros(acc_scr.shape, F32)

    a = jnp.dot(u_ref[...], ht_scr[...], preferred_element_type=F32)
    act = 0.5 * a * (1.0 + lax.erf(a * (1.0 / math.sqrt(2.0))))
    parts = []
    for gi in range(groups):
        i = kb * groups + gi
        w = jnp.zeros((nk, t), F32)
        for h in range(heads):
            th = th_ref[h, pl.ds(i, 1), :]
            e1 = e1_ref[h, pl.ds(i, 1), :]
            w = w + jnp.where(s2_ref[h] >= th, e2_ref[h], 0.0) * e1
        parts.append((w * act[gi * nk:(gi + 1) * nk, :]).astype(BF16))
    gmat = jnp.concatenate(parts, axis=0) if groups > 1 else parts[0]
    acc_scr[...] += jnp.dot(vt_ref[...], gmat, preferred_element_type=F32)

    @pl.when(kb == pl.num_programs(1) - 1)
    def _():
        o_ref[...] = x_ref[...] + gt_ref[...] * acc_scr[...].T


def _peer_call(x2, g, sh, sc, gt, th, e1, s2, e2, u, vt, rows_per_mod):
    m, d = x2.shape
    heads, nk, _ = th.shape
    e = u.shape[0]
    t = _tile(rows_per_mod, 512)
    per = rows_per_mod // t
    eb = _tile(e, 512)
    sel = pl.BlockSpec((heads, nk, t), lambda i, k: (0, 0, i))
    mod = pl.BlockSpec((None, 1, d), lambda i, k: (i // per, 0, 0))
    return pl.pallas_call(
        _peer_kernel,
        grid=(m // t, e // eb),
        in_specs=[pl.BlockSpec((t, d), lambda i, k: (i, 0)),
                  pl.BlockSpec((1, d), lambda i, k: (0, 0)),
                  mod, mod, mod, sel, sel, sel, sel,
                  pl.BlockSpec((eb, d), lambda i, k: (k, 0)),
                  pl.BlockSpec((d, eb), lambda i, k: (0, k))],
        out_specs=pl.BlockSpec((t, d), lambda i, k: (i, 0)),
        out_shape=jax.ShapeDtypeStruct((m, d), F32),
        scratch_shapes=[pltpu.VMEM((d, t), BF16), pltpu.VMEM((d, t), F32)],
        compiler_params=_params(("parallel", "arbitrary")),
        name="peer_experts",
    )(x2, g, sh, sc, gt, th, e1, s2, e2, u, vt)


def _final_kernel(x_ref, g_ref, o_ref):
    x = x_ref[...]
    o_ref[...] = x * lax.rsqrt(jnp.mean(x * x, axis=-1, keepdims=True) + EPS) * g_ref[...]


def _final_call(x2, g):
    m, d = x2.shape
    tm = _tile(m, 1024)
    return pl.pallas_call(
        _final_kernel,
        grid=(m // tm,),
        in_specs=[pl.BlockSpec((tm, d), lambda i: (i, 0)), pl.BlockSpec((1, d), lambda i: (0, 0))],
        out_specs=pl.BlockSpec((tm, d), lambda i: (i, 0)),
        out_shape=jax.ShapeDtypeStruct((m, d), F32),
        compiler_params=_params(("parallel",)),
        name="final_norm",
    )(x2, g)


def _peer_layer(x2, g_ffn, sh, sc, gt, w_q, keys, u, v, seq):
    d = x2.shape[1]
    qk = _nmm_call(x2, g_ffn, sh, sc, w_q.astype(BF16), seq, BF16)
    kb = keys.reshape(2 * PEER_HEADS, PEER_NKEYS, PEER_DKEY // 2).astype(BF16)
    th, e1, s2, e2 = _topk_call(qk, kb)
    return _peer_call(x2, g_ffn, sh, sc, gt, th, e1, s2, e2,
                      u.astype(BF16), v.astype(BF16).T, seq)


def kernel(x, c, ctx, c_ctx, w_ada, b_ada, g_mix, g_ffn, hg_w_in, hg_w_out, hg_gnorm, hg_lb,
           cv_w_pw1, cv_b_pw1, cv_w_dw, cv_b_dw, cv_ln_g, cv_ln_b, cv_w_pw2, cv_b_pw2,
           peer_w_q, peer_keys, peer_u, peer_v, g_final):
    b, seq, d = x.shape
    ctx_len = ctx.shape[1]
    depth = w_ada.shape[0]
    heads = d // LANES
    assert heads * LANES == d and seq % SCAN_CHUNK == 0 and ctx_len % SCAN_CHUNK == 0

    pad_rows = (-(b + 1)) % SUBLANES
    cc = jnp.concatenate([c, c_ctx[None, :], jnp.zeros((pad_rows, d), F32)], axis=0)
    ada = _ada_call(cc, w_ada, b_ada)

    x2 = x.reshape(b * seq, d)
    for l in range(depth):
        mixer = l % N_MIXERS
        mods = ada[l, :b].reshape(b, 1, 6, d)
        sh_m, sc_m, gt_m, sh_f, sc_f, gt_f = [mods[:, :, i, :] for i in range(6)]
        g_mix_l = g_mix[l][None, :]
        g_ffn_l = g_ffn[l][None, :]
        if mixer == 0:
            a = l // N_MIXERS
            w_in = hg_w_in[a].astype(BF16)
            cmods = ada[l, b].reshape(6, d)
            csh = jnp.broadcast_to(cmods[0][None, None, :], (b, 1, d))
            csc = jnp.broadcast_to(cmods[1][None, None, :], (b, 1, d))
            zc = _nmm_call(ctx.reshape(b * ctx_len, d), g_mix_l, csh, csc, w_in, ctx_len, F32)
            s0 = jnp.zeros((b, heads, LANES, LANES), F32)
            _, _, s_f, s_b = _scan_call(zc.reshape(b, ctx_len, 5 * d), hg_lb, s0, s0, l)
            z = _nmm_call(x2, g_mix_l, sh_m, sc_m, w_in, seq, F32)
            o_f, o_b, _, _ = _scan_call(z.reshape(b, seq, 5 * d), hg_lb, s_f, s_b, l)
            x2 = _mix_out_call(o_f.reshape(b * seq, d), o_b.reshape(b * seq, d), z,
                               hg_gnorm[a][None, :], hg_w_out[a].astype(BF16), x2, gt_m, seq)
        else:
            bi = l // N_MIXERS
            y = _nmm_call(x2, g_mix_l, sh_m, sc_m, cv_w_pw1[bi].astype(BF16), seq, F32,
                          glu_bias=cv_b_pw1[bi][None, :])
            y = _conv_call(y.reshape(b, seq, d), cv_w_dw[bi])
            x2 = _ln_out_call(y.reshape(b * seq, d), cv_b_dw[bi][None, :], cv_ln_g[bi][None, :],
                              cv_ln_b[bi][None, :], cv_w_pw2[bi].astype(BF16),
                              cv_b_pw2[bi][None, :], x2, gt_m, seq)
        x2 = _peer_layer(x2, g_ffn_l, sh_f, sc_f, gt_f, peer_w_q[l], peer_keys[l],
                         peer_u[l], peer_v[l], seq)
    return _final_call(x2, g_final[None, :]).reshape(b, seq, d)
```

```python
import functools
import math

import numpy as np
import jax
import jax.numpy as jnp
from jax import lax
from jax.experimental import pallas as pl
from jax.experimental.pallas import tpu as pltpu

EPS = 1e-6
GRID_W = 64
CONV_WIDTH = 31
HG_HEADS = 8
PEER_HEADS = 8
PEER_NKEYS = 128
PEER_DKEY = 256
PEER_TOPK = 16
N_MIXERS = 2

LANES = 128
SUBLANES = 8
SCAN_CHUNK = 128
SCAN_MATMUL_LEVELS = int(math.log2(SUBLANES))
PEER_EB = 512
PEER_SUB = 256
VMEM_LIMIT = 56 * 1024 * 1024

F32 = jnp.float32
BF16 = jnp.bfloat16


def _tile(n, pref):
    t = min(n, pref)
    while n % t:
        t //= 2
    return t


def _params(sem):
    return pltpu.CompilerParams(dimension_semantics=sem, vmem_limit_bytes=VMEM_LIMIT)


def _rmsnorm_mod(x, g, sh, sc):
    y = x * lax.rsqrt(jnp.mean(x * x, axis=-1, keepdims=True) + EPS) * g
    return y * (1.0 + sc) + sh


def _ada_kernel(c_ref, w_ref, b_ref, o_ref):
    s = c_ref[...]
    s = s * jax.nn.sigmoid(s)
    o_ref[...] = jnp.dot(s, w_ref[...], preferred_element_type=F32,
                         precision=lax.Precision.HIGHEST) + b_ref[...]


def _ada_call(cc, w_ada, b_ada):
    depth, d, n = w_ada.shape
    r = cc.shape[0]
    tn = _tile(n, 1024)
    return pl.pallas_call(
        _ada_kernel,
        grid=(depth, n // tn),
        in_specs=[pl.BlockSpec((r, d), lambda l, j: (0, 0)),
                  pl.BlockSpec((None, d, tn), lambda l, j: (l, 0, j)),
                  pl.BlockSpec((None, 1, tn), lambda l, j: (l, 0, j))],
        out_specs=pl.BlockSpec((None, r, tn), lambda l, j: (l, 0, j)),
        out_shape=jax.ShapeDtypeStruct((depth, r, n), F32),
        compiler_params=_params(("parallel", "parallel")),
        name="ada",
    )(cc, w_ada, b_ada.reshape(depth, 1, n))


def _nmm_kernel(x_ref, g_ref, sh_ref, sc_ref, w_ref, o_ref, h_scr):
    @pl.when(pl.program_id(1) == 0)
    def _():
        h_scr[...] = _rmsnorm_mod(x_ref[...], g_ref[...], sh_ref[...], sc_ref[...]).astype(BF16)

    o_ref[...] = jnp.dot(h_scr[...], w_ref[...], preferred_element_type=F32).astype(o_ref.dtype)


def _nmm_glu_kernel(x_ref, g_ref, sh_ref, sc_ref, wa_ref, wg_ref, ba_ref, bg_ref, o_ref, h_scr):
    @pl.when(pl.program_id(1) == 0)
    def _():
        h_scr[...] = _rmsnorm_mod(x_ref[...], g_ref[...], sh_ref[...], sc_ref[...]).astype(BF16)

    h = h_scr[...]
    a = jnp.dot(h, wa_ref[...], preferred_element_type=F32) + ba_ref[...]
    g = jnp.dot(h, wg_ref[...], preferred_element_type=F32) + bg_ref[...]
    o_ref[...] = (a * jax.nn.sigmoid(g)).astype(o_ref.dtype)


def _nmm_call(x2, g, sh, sc, w, rows_per_mod, out_dtype, glu_bias=None):
    m, d = x2.shape
    n = w.shape[1]
    tm = _tile(rows_per_mod, 1024)
    per = rows_per_mod // tm
    x_spec = pl.BlockSpec((tm, d), lambda i, j: (i, 0))
    g_spec = pl.BlockSpec((1, d), lambda i, j: (0, 0))
    mod_spec = pl.BlockSpec((None, 1, d), lambda i, j: (i // per, 0, 0))
    scratch = [pltpu.VMEM((tm, d), BF16)]
    if glu_bias is None:
        tn = _tile(n, 1024)
        return pl.pallas_call(
            _nmm_kernel,
            grid=(m // tm, n // tn),
            in_specs=[x_spec, g_spec, mod_spec, mod_spec,
                      pl.BlockSpec((d, tn), lambda i, j: (0, j))],
            out_specs=pl.BlockSpec((tm, tn), lambda i, j: (i, j)),
            out_shape=jax.ShapeDtypeStruct((m, n), out_dtype),
            scratch_shapes=scratch,
            compiler_params=_params(("parallel", "arbitrary")),
            name="norm_mod_matmul",
        )(x2, g, sh, sc, w)
    nh = n // 2
    tn = _tile(nh, 512)
    nb = nh // tn
    return pl.pallas_call(
        _nmm_glu_kernel,
        grid=(m // tm, nb),
        in_specs=[x_spec, g_spec, mod_spec, mod_spec,
                  pl.BlockSpec((d, tn), lambda i, j: (0, j)),
                  pl.BlockSpec((d, tn), lambda i, j: (0, j + nb)),
                  pl.BlockSpec((1, tn), lambda i, j: (0, j)),
                  pl.BlockSpec((1, tn), lambda i, j: (0, j + nb))],
        out_specs=pl.BlockSpec((tm, tn), lambda i, j: (i, j)),
        out_shape=jax.ShapeDtypeStruct((m, nh), out_dtype),
        scratch_shapes=scratch,
        compiler_params=_params(("parallel", "arbitrary")),
        name="norm_mod_matmul_glu",
    )(x2, g, sh, sc, w, w, glu_bias, glu_bias)


@functools.lru_cache(maxsize=None)
def _scan_consts(c):
    nl = int(math.log2(c))
    assert 1 << nl == c
    idx = np.arange(c)
    mats = np.zeros((2, 2 + SCAN_MATMUL_LEVELS, c, c), np.float32)
    level = np.zeros((2, c, c), np.int32)
    for d in range(2):
        tau = idx if d == 0 else c - 1 - idx
        tt, tr = tau[:, None], tau[None, :]
        mats[d, 0] = tr <= tt
        mats[d, 1] = tr > tt
        for l in range(SCAN_MATMUL_LEVELS):
            half = 1 << l
            mid = (tau // (2 * half)) * (2 * half) + half - 1
            later = ((tau >> l) & 1) == 1
            mats[d, 2 + l] = ((later[:, None] & (tr > mid[:, None]) & (tr <= tt))
                              | ((~later)[:, None] & (tr > tt) & (tr <= mid[:, None])))
        xor = tt ^ tr
        lev = np.floor(np.log2(np.maximum(xor, 1))).astype(np.int32)
        level[d] = np.where(tt > tr, lev, np.where(tt == tr, -1, -2))
    return mats.reshape(2, (2 + SCAN_MATMUL_LEVELS) * c, c), level, nl


def _scan_direction(d, zf, v, q, lb, est_ref, lev_ref, st_ref, o_ref, n_levels):
    c = zf.shape[0]
    heads = st_ref.shape[0]
    f = lb + (1.0 - lb) * jax.nn.sigmoid(zf)
    logf = jnp.log(f)
    kk = (1.0 - lb) * jax.nn.sigmoid(-zf)
    lev = lev_ref[d]
    for h in range(heads):
        sl = slice(h * LANES, (h + 1) * LANES)
        lf = logf[:, sl]
        hi = lf.astype(BF16)
        mid = (lf - hi.astype(F32)).astype(BF16)
        r = jnp.dot(est_ref[d], jnp.concatenate([hi, mid], axis=1), preferred_element_type=F32)
        ex = r[:, :LANES] + r[:, LANES:]
        p = jnp.exp(ex)
        dec = jnp.exp(ex[0:1, :] + ex[c:c + 1, :])
        q_h, k_h, v_h = q[:, sl], kk[:, sl], v[:, sl]
        st = st_ref[h]
        qb = (q_h * p[0:c]).astype(BF16)
        inter = lax.dot_general(qb, st.astype(BF16), (((1,), (1,)), ((), ())),
                                preferred_element_type=F32)
        pr = lax.dot_general(q_h.astype(BF16), k_h.astype(BF16), (((1,), (1,)), ((), ())),
                             preferred_element_type=F32)
        scores = jnp.where(lev == -1, pr, 0.0)
        b = ex[0:c]
        for l in range(n_levels):
            if l < SCAN_MATMUL_LEVELS:
                pl_ = p[(2 + l) * c:(3 + l) * c]
            else:
                half = 1 << l
                rows = [jnp.broadcast_to(b[p0 + half - 1 + d:p0 + half + d, :], (2 * half, LANES))
                        for p0 in range(0, c, 2 * half)]
                bmid = jnp.concatenate(rows, axis=0) if len(rows) > 1 else rows[0]
                pl_ = jnp.exp(-jnp.abs(b - bmid))
            ql = (q_h * pl_).astype(BF16)
            kl = (k_h * pl_).astype(BF16)
            pr = lax.dot_general(ql, kl, (((1,), (1,)), ((), ())), preferred_element_type=F32)
            scores = jnp.where(lev == l, pr, scores)
        intra = jnp.dot(scores.astype(BF16), v_h.astype(BF16), preferred_element_type=F32)
        o_ref[:, sl] = inter + intra
        kend = (k_h * p[c:2 * c]).astype(BF16)
        st_ref[h] = st * dec + jnp.dot(v_h.T.astype(BF16), kend, preferred_element_type=F32)


def _scan_kernel(lbraw_ref, est_ref, lev_ref, s0f_ref, s0b_ref,
                 vf_ref, ff_ref, qf_ref, vb_ref, fb_ref, qb_ref,
                 of_ref, ob_ref, sf_ref, sb_ref, stf_scr, stb_scr, *, layer, n_levels):
    j = pl.program_id(1)

    @pl.when(j == 0)
    def _():
        stf_scr[...] = s0f_ref[...]
        stb_scr[...] = s0b_ref[...]

    raw = lbraw_ref[...]
    e = jnp.exp(raw - jnp.max(raw, axis=0, keepdims=True))
    lb_all = jnp.sum(e[0:layer + 1], axis=0, keepdims=True) / jnp.sum(e, axis=0, keepdims=True)
    d_model = vf_ref.shape[-1]
    _scan_direction(0, ff_ref[...], vf_ref[...], qf_ref[...], lb_all[:, :d_model],
                    est_ref, lev_ref, stf_scr, of_ref, n_levels)
    _scan_direction(1, fb_ref[...], vb_ref[...], qb_ref[...], lb_all[:, d_model:],
                    est_ref, lev_ref, stb_scr, ob_ref, n_levels)

    @pl.when(j == pl.num_programs(1) - 1)
    def _():
        sf_ref[...] = stf_scr[...]
        sb_ref[...] = stb_scr[...]


def _scan_call(z, hg_lb, s0f, s0b, layer):
    b, l, d5 = z.shape
    d = d5 // 5
    heads = d // LANES
    c = SCAN_CHUNK
    nc = l // c
    est, lev, n_levels = _scan_consts(c)
    est = jnp.asarray(est, BF16)
    lev = jnp.asarray(lev)

    def zspec(col, rev):
        if rev:
            return pl.BlockSpec((None, c, d), lambda bi, j: (bi, nc - 1 - j, col))
        return pl.BlockSpec((None, c, d), lambda bi, j: (bi, j, col))

    full = lambda a: pl.BlockSpec(a.shape, lambda bi, j: (0,) * a.ndim)
    st_spec = pl.BlockSpec((None, heads, LANES, LANES), lambda bi, j: (bi, 0, 0, 0))
    return pl.pallas_call(
        functools.partial(_scan_kernel, layer=layer, n_levels=n_levels),
        grid=(b, nc),
        in_specs=[full(hg_lb), full(est), full(lev), st_spec, st_spec,
                  zspec(0, False), zspec(1, False), zspec(3, False),
                  zspec(0, True), zspec(2, True), zspec(3, True)],
        out_specs=[pl.BlockSpec((None, c, d), lambda bi, j: (bi, j, 0)),
                   pl.BlockSpec((None, c, d), lambda bi, j: (bi, nc - 1 - j, 0)),
                   st_spec, st_spec],
        out_shape=[jax.ShapeDtypeStruct((b, l, d), F32), jax.ShapeDtypeStruct((b, l, d), F32),
                   jax.ShapeDtypeStruct(s0f.shape, F32), jax.ShapeDtypeStruct(s0b.shape, F32)],
        scratch_shapes=[pltpu.VMEM((heads, LANES, LANES), F32), pltpu.VMEM((heads, LANES, LANES), F32)],
        compiler_params=_params(("parallel", "arbitrary")),
        name="hgrn2_scan",
    )(hg_lb, est, lev, s0f, s0b, z, z, z, z, z, z)


def _mix_out_kernel(of_ref, ob_ref, gate_ref, gn_ref, w_ref, x_ref, gt_ref, o_ref, h_scr):
    @pl.when(pl.program_id(1) == 0)
    def _():
        o = of_ref[...] + ob_ref[...]
        gate = gate_ref[...]
        gn = gn_ref[...]
        for h in range(o.shape[1] // LANES):
            sl = slice(h * LANES, (h + 1) * LANES)
            oh = o[:, sl]
            y = oh * lax.rsqrt(jnp.mean(oh * oh, axis=-1, keepdims=True) + EPS) * gn
            gh = gate[:, sl]
            h_scr[:, sl] = (y * (gh * jax.nn.sigmoid(gh))).astype(BF16)

    mix = jnp.dot(h_scr[...], w_ref[...], preferred_element_type=F32)
    o_ref[...] = x_ref[...] + gt_ref[...] * mix


def _mix_out_call(o_f, o_b, z2, gnorm, w_out, x2, gt, rows_per_mod):
    m, d = x2.shape
    tm = _tile(rows_per_mod, 1024)
    per = rows_per_mod // tm
    tn = _tile(d, 512)
    row = pl.BlockSpec((tm, d), lambda i, j: (i, 0))
    return pl.pallas_call(
        _mix_out_kernel,
        grid=(m // tm, d // tn),
        in_specs=[row, row,
                  pl.BlockSpec((tm, d), lambda i, j: (i, 4)),
                  pl.BlockSpec((1, LANES), lambda i, j: (0, 0)),
                  pl.BlockSpec((d, tn), lambda i, j: (0, j)),
                  pl.BlockSpec((tm, tn), lambda i, j: (i, j)),
                  pl.BlockSpec((None, 1, tn), lambda i, j: (i // per, 0, j))],
        out_specs=pl.BlockSpec((tm, tn), lambda i, j: (i, j)),
        out_shape=jax.ShapeDtypeStruct((m, d), F32),
        scratch_shapes=[pltpu.VMEM((tm, d), BF16)],
        compiler_params=_params(("parallel", "arbitrary")),
        name="hgrn2_out",
    )(o_f, o_b, z2, gnorm, w_out, x2, gt)


def _conv_kernel(y_ref, w_ref, o_ref, padw_scr, padh_scr, *, rows, n_w_tiles):
    ct = pl.program_id(1)
    half = (CONV_WIDTH - 1) // 2
    l, lanes = y_ref.shape
    lead = padw_scr.shape[1] - GRID_W - 16

    @pl.when(ct < n_w_tiles)
    def _():
        padw_scr[...] = jnp.zeros(padw_scr.shape, F32)
        padw_scr[:, lead:lead + GRID_W, :] = y_ref[...].reshape(rows, GRID_W, lanes)
        acc = jnp.zeros((rows, GRID_W, lanes), F32)
        for k in range(CONV_WIDTH):
            start = lead + k - half
            acc = acc + w_ref[k:k + 1, :].reshape(1, 1, lanes) * padw_scr[:, start:start + GRID_W, :]
        o_ref[...] = acc.reshape(l, lanes)

    @pl.when(ct >= n_w_tiles)
    def _():
        margin = half * GRID_W
        padh_scr[...] = jnp.zeros(padh_scr.shape, F32)
        padh_scr[margin:margin + l, :] = y_ref[...]
        acc = jnp.zeros((l, lanes), F32)
        for k in range(CONV_WIDTH):
            acc = acc + w_ref[k:k + 1, :] * padh_scr[k * GRID_W:k * GRID_W + l, :]
        o_ref[...] = acc


def _conv_call(y, w_dw):
    b, l, c = y.shape
    rows = l // GRID_W
    half = (CONV_WIDTH - 1) // 2
    n_w_tiles = (c // 2) // LANES
    wp = jnp.pad(w_dw, ((0, 32 - CONV_WIDTH), (0, 0)))
    return pl.pallas_call(
        functools.partial(_conv_kernel, rows=rows, n_w_tiles=n_w_tiles),
        grid=(b, c // LANES),
        in_specs=[pl.BlockSpec((None, l, LANES), lambda bi, ct: (bi, 0, ct)),
                  pl.BlockSpec((32, LANES), lambda bi, ct: (0, ct))],
        out_specs=pl.BlockSpec((None, l, LANES), lambda bi, ct: (bi, 0, ct)),
        out_shape=jax.ShapeDtypeStruct((b, l, c), F32),
        scratch_shapes=[pltpu.VMEM((rows, GRID_W + 32, LANES), F32),
                        pltpu.VMEM((l + 2 * half * GRID_W, LANES), F32)],
        compiler_params=_params(("parallel", "parallel")),
        name="axial_dwconv",
    )(y, wp)


def _ln_out_kernel(y_ref, bdw_ref, lg_ref, lb_ref, w_ref, b2_ref, x_ref, gt_ref, o_ref, h_scr):
    @pl.when(pl.program_id(1) == 0)
    def _():
        y = y_ref[...] + bdw_ref[...]
        yc = y - jnp.mean(y, axis=-1, keepdims=True)
        yn = yc * lax.rsqrt(jnp.mean(yc * yc, axis=-1, keepdims=True) + EPS)
        yn = yn * lg_ref[...] + lb_ref[...]
        h_scr[...] = (yn * jax.nn.sigmoid(yn)).astype(BF16)

    mix = jnp.dot(h_scr[...], w_ref[...], preferred_element_type=F32) + b2_ref[...]
    o_ref[...] = x_ref[...] + gt_ref[...] * mix


def _ln_out_call(y2, b_dw, ln_g, ln_b, w_pw2, b_pw2, x2, gt, rows_per_mod):
    m, d = x2.shape
    tm = _tile(rows_per_mod, 1024)
    per = rows_per_mod // tm
    tn = _tile(d, 512)
    vec = pl.BlockSpec((1, d), lambda i, j: (0, 0))
    return pl.pallas_call(
        _ln_out_kernel,
        grid=(m // tm, d // tn),
        in_specs=[pl.BlockSpec((tm, d), lambda i, j: (i, 0)), vec, vec, vec,
                  pl.BlockSpec((d, tn), lambda i, j: (0, j)),
                  pl.BlockSpec((1, tn), lambda i, j: (0, j)),
                  pl.BlockSpec((tm, tn), lambda i, j: (i, j)),
                  pl.BlockSpec((None, 1, tn), lambda i, j: (i // per, 0, j))],
        out_specs=pl.BlockSpec((tm, tn), lambda i, j: (i, j)),
        out_shape=jax.ShapeDtypeStruct((m, d), F32),
        scratch_shapes=[pltpu.VMEM((tm, d), BF16)],
        compiler_params=_params(("parallel", "arbitrary")),
        name="conv_out",
    )(y2, b_dw, ln_g, ln_b, w_pw2, b_pw2, x2, gt)


def _vmax(a, b):
    if a is None:
        return b
    if b is None:
        return a
    return jnp.maximum(a, b)


def _vmin(a, b):
    if a is None or b is None:
        return None
    return jnp.minimum(a, b)


def _bitonic_merge_desc(xs):
    n = len(xs)
    j = n // 2
    while j >= 1:
        for i in range(n):
            o = i ^ j
            if o > i:
                a, b = xs[i], xs[o]
                xs[i], xs[o] = _vmax(a, b), _vmin(a, b)
        j //= 2
    return xs


def _bitonic_sort_desc(xs):
    n = len(xs)
    k = 2
    while k <= n:
        j = k // 2
        while j >= 1:
            for i in range(n):
                o = i ^ j
                if o > i:
                    a, b = xs[i], xs[o]
                    if (i & k) == 0:
                        xs[i], xs[o] = _vmax(a, b), _vmin(a, b)
                    else:
                        xs[i], xs[o] = _vmin(a, b), _vmax(a, b)
            j //= 2
        k *= 2
    return xs


def _merge_top(a, b):
    n = len(a)
    b = list(b) + [None] * (n - len(b))
    return _bitonic_merge_desc([_vmax(a[i], b[n - 1 - i]) for i in range(n)])


def _topk_kernel(q_ref, keys_ref, th_ref, e1_ref, s2_ref, e2_ref, s_scr):
    heads = th_ref.shape[0]
    nk = keys_ref.shape[1]
    k = PEER_TOPK
    t = q_ref.shape[0]
    sub = lax.broadcasted_iota(jnp.int32, (SUBLANES, t), 0)
    tops = [[None] * k, [None] * k]
    for h in range(heads):
        for p in range(2):
            hp = 2 * h + p
            dk = keys_ref.shape[2]
            s = lax.dot_general(keys_ref[hp], q_ref[:, hp * dk:(hp + 1) * dk],
                                (((1,), (1,)), ((), ())), preferred_element_type=F32)
            s_scr[hp] = s
            groups = _bitonic_sort_desc([s[SUBLANES * a:SUBLANES * (a + 1), :] for a in range(nk // SUBLANES)])
            assert len(groups) == k
            for shift in (4, 2, 1):
                groups = _merge_top(groups, [pltpu.roll(g, shift, 0) for g in groups])
            for a in range(k):
                prev = tops[p][a]
                tops[p][a] = groups[a] if prev is None else jnp.where(sub == h, groups[a], prev)
    v1, v2 = tops
    best = [v1[0] + v2[b] for b in range(k)]
    for a in range(1, k // 2):
        best = _merge_top(best, [v1[a] + v2[b] for b in range(k // (a + 1))])
    best = _merge_top(best, [v1[a] + v2[0] for a in range(k // 2, k)])
    cmax = best[0]
    zsum = jnp.ones_like(cmax)
    for a in range(1, k):
        zsum = zsum + jnp.exp(best[a] - cmax)
    tau = best[k - 1]
    inv_z = 1.0 / zsum
    for h in range(heads):
        s1 = s_scr[2 * h]
        s2 = s_scr[2 * h + 1]
        tau_h = tau[h:h + 1, :]
        slack = (jnp.abs(tau_h) + jnp.abs(s1)) * (2.0 ** -22)
        th_ref[h] = (tau_h - s1) - slack
        e1_ref[h] = jnp.exp(s1 - v1[0][h:h + 1, :])
        s2_ref[h] = s2
        e2_ref[h] = jnp.exp(s2 - v2[0][h:h + 1, :]) * inv_z[h:h + 1, :]


def _topk_call(qk, keys):
    m = qk.shape[0]
    hp, nk, dk = keys.shape
    heads = hp // 2
    t = _tile(m, 512)
    out = jax.ShapeDtypeStruct((heads, nk, m), F32)
    ospec = pl.BlockSpec((heads, nk, t), lambda i: (0, 0, i))
    return pl.pallas_call(
        _topk_kernel,
        grid=(m // t,),
        in_specs=[pl.BlockSpec((t, hp * dk), lambda i: (i, 0)),
                  pl.BlockSpec((hp, nk, dk), lambda i: (0, 0, 0))],
        out_specs=[ospec, ospec, ospec, ospec],
        out_shape=[out, out, out, out],
        scratch_shapes=[pltpu.VMEM((hp, nk, t), F32)],
        compiler_params=_params(("parallel",)),
        name="peer_topk",
    )(qk, keys)


def _peer_weights(i, a, g_out, row0, bc_scr, th_ref, e1_ref, s2_ref, e2_ref):
    heads, nk, t = th_ref.shape
    for h in range(heads):
        bc_scr[0, h] = jnp.broadcast_to(th_ref[h, pl.ds(i, 1), :], (SUBLANES, t))
        bc_scr[1, h] = jnp.broadcast_to(e1_ref[h, pl.ds(i, 1), :], (SUBLANES, t))
    for tl in range(t // LANES):
        lanes = slice(tl * LANES, (tl + 1) * LANES)
        w = None
        for h in range(heads):
            th = pltpu.repeat(bc_scr[0, h, :, lanes], nk // SUBLANES, axis=0)
            e1 = pltpu.repeat(bc_scr[1, h, :, lanes], nk // SUBLANES, axis=0)
            term = jnp.where(s2_ref[h, :, lanes] >= th, e2_ref[h, :, lanes], 0.0) * e1
            w = term if w is None else w + term
        a_t = a[:, lanes]
        act = 0.5 * a_t * (1.0 + lax.erf(a_t * (1.0 / math.sqrt(2.0))))
        g_out[row0:row0 + nk, lanes] = (w * act).astype(BF16)


def _peer_kernel(x_ref, g_ref, sh_ref, sc_ref, gt_ref, th_ref, e1_ref, s2_ref, e2_ref,
                 u_ref, vtp_ref, vtc_ref, vtl_ref, o_ref, ht_scr, ga_scr, gb_scr, acc_scr, bc_scr):
    kb = pl.program_id(1)
    nk = th_ref.shape[1]
    n_sub = PEER_EB // PEER_SUB
    sel = (th_ref, e1_ref, s2_ref, e2_ref)

    @pl.when(kb == 0)
    def _():
        h = _rmsnorm_mod(x_ref[...], g_ref[...], sh_ref[...], sc_ref[...])
        ht_scr[...] = h.T.astype(BF16)
        gb_scr[...] = jnp.zeros(gb_scr.shape, BF16)
        acc_scr[...] = jnp.zeros(acc_scr.shape, F32)

    def first_matmul(idx):
        rows = slice(idx * PEER_SUB, (idx + 1) * PEER_SUB)
        return jnp.dot(u_ref[rows, :], ht_scr[...], preferred_element_type=F32)

    a_next = first_matmul(0)
    for idx in range(2 * n_sub):
        phase, sb = divmod(idx, n_sub)
        a = a_next
        if idx + 1 < 2 * n_sub:
            a_next = first_matmul(idx + 1)
        if sb == min(1, n_sub - 1):
            vt_ref, g_in = (vtp_ref, gb_scr) if phase == 0 else (vtc_ref, ga_scr)
            acc_scr[...] += jnp.dot(vt_ref[...], g_in[...], preferred_element_type=F32)
        g_out = ga_scr if phase == 0 else gb_scr
        for gi in range(PEER_SUB // nk):
            i = kb * (2 * PEER_EB // nk) + idx * (PEER_SUB // nk) + gi
            _peer_weights(i, a[gi * nk:(gi + 1) * nk, :], g_out, sb * PEER_SUB + gi * nk, bc_scr, *sel)

    @pl.when(kb == pl.num_programs(1) - 1)
    def _():
        acc = acc_scr[...] + jnp.dot(vtl_ref[...], gb_scr[...], preferred_element_type=F32)
        o_ref[...] = x_ref[...] + gt_ref[...] * acc.T


def _peer_call(x2, g, sh, sc, gt, th, e1, s2, e2, u, vt, rows_per_mod):
    m, d = x2.shape
    heads, nk, _ = th.shape
    e = u.shape[0]
    t = _tile(rows_per_mod, 512)
    per = rows_per_mod // t
    eb = PEER_EB
    nkb = e // (2 * eb)
    assert nkb * 2 * eb == e
    sel = pl.BlockSpec((heads, nk, t), lambda i, k: (0, 0, i))
    mod = pl.BlockSpec((None, 1, d), lambda i, k: (i // per, 0, 0))
    return pl.pallas_call(
        _peer_kernel,
        grid=(m // t, nkb),
        in_specs=[pl.BlockSpec((t, d), lambda i, k: (i, 0)),
                  pl.BlockSpec((1, d), lambda i, k: (0, 0)),
                  mod, mod, mod, sel, sel, sel, sel,
                  pl.BlockSpec((2 * eb, d), lambda i, k: (k, 0)),
                  pl.BlockSpec((d, eb), lambda i, k: (0, jnp.maximum(2 * k - 1, 0))),
                  pl.BlockSpec((d, eb), lambda i, k: (0, 2 * k)),
                  pl.BlockSpec((d, eb), lambda i, k: (0, 2 * nkb - 1))],
        out_specs=pl.BlockSpec((t, d), lambda i, k: (i, 0)),
        out_shape=jax.ShapeDtypeStruct((m, d), F32),
        scratch_shapes=[pltpu.VMEM((d, t), BF16), pltpu.VMEM((eb, t), BF16),
                        pltpu.VMEM((eb, t), BF16), pltpu.VMEM((d, t), F32),
                        pltpu.VMEM((2, heads, SUBLANES, t), F32)],
        compiler_params=_params(("parallel", "arbitrary")),
        name="peer_experts",
    )(x2, g, sh, sc, gt, th, e1, s2, e2, u, vt, vt, vt)


def _final_kernel(x_ref, g_ref, o_ref):
    x = x_ref[...]
    o_ref[...] = x * lax.rsqrt(jnp.mean(x * x, axis=-1, keepdims=True) + EPS) * g_ref[...]


def _final_call(x2, g):
    m, d = x2.shape
    tm = _tile(m, 1024)
    return pl.pallas_call(
        _final_kernel,
        grid=(m // tm,),
        in_specs=[pl.BlockSpec((tm, d), lambda i: (i, 0)), pl.BlockSpec((1, d), lambda i: (0, 0))],
        out_specs=pl.BlockSpec((tm, d), lambda i: (i, 0)),
        out_shape=jax.ShapeDtypeStruct((m, d), F32),
        compiler_params=_params(("parallel",)),
        name="final_norm",
    )(x2, g)


def _peer_layer(x2, g_ffn, sh, sc, gt, w_q, keys, u, v, seq):
    d = x2.shape[1]
    qk = _nmm_call(x2, g_ffn, sh, sc, w_q.astype(BF16), seq, BF16)
    kb = keys.reshape(2 * PEER_HEADS, PEER_NKEYS, PEER_DKEY // 2).astype(BF16)
    th, e1, s2, e2 = _topk_call(qk, kb)
    return _peer_call(x2, g_ffn, sh, sc, gt, th, e1, s2, e2,
                      u.astype(BF16), v.astype(BF16).T, seq)


def kernel(x, c, ctx, c_ctx, w_ada, b_ada, g_mix, g_ffn, hg_w_in, hg_w_out, hg_gnorm, hg_lb,
           cv_w_pw1, cv_b_pw1, cv_w_dw, cv_b_dw, cv_ln_g, cv_ln_b, cv_w_pw2, cv_b_pw2,
           peer_w_q, peer_keys, peer_u, peer_v, g_final):
    b, seq, d = x.shape
    ctx_len = ctx.shape[1]
    depth = w_ada.shape[0]
    heads = d // LANES
    assert heads * LANES == d and seq % SCAN_CHUNK == 0 and ctx_len % SCAN_CHUNK == 0

    pad_rows = (-(b + 1)) % SUBLANES
    cc = jnp.concatenate([c, c_ctx[None, :], jnp.zeros((pad_rows, d), F32)], axis=0)
    ada = _ada_call(cc, w_ada, b_ada)

    x2 = x.reshape(b * seq, d)
    for l in range(depth):
        mixer = l % N_MIXERS
        mods = ada[l, :b].reshape(b, 1, 6, d)
        sh_m, sc_m, gt_m, sh_f, sc_f, gt_f = [mods[:, :, i, :] for i in range(6)]
        g_mix_l = g_mix[l][None, :]
        g_ffn_l = g_ffn[l][None, :]
        if mixer == 0:
            a = l // N_MIXERS
            w_in = hg_w_in[a].astype(BF16)
            cmods = ada[l, b].reshape(6, d)
            csh = jnp.broadcast_to(cmods[0][None, None, :], (b, 1, d))
            csc = jnp.broadcast_to(cmods[1][None, None, :], (b, 1, d))
            zc = _nmm_call(ctx.reshape(b * ctx_len, d), g_mix_l, csh, csc, w_in, ctx_len, F32)
            s0 = jnp.zeros((b, heads, LANES, LANES), F32)
            _, _, s_f, s_b = _scan_call(zc.reshape(b, ctx_len, 5 * d), hg_lb, s0, s0, l)
            z = _nmm_call(x2, g_mix_l, sh_m, sc_m, w_in, seq, F32)
            o_f, o_b, _, _ = _scan_call(z.reshape(b, seq, 5 * d), hg_lb, s_f, s_b, l)
            x2 = _mix_out_call(o_f.reshape(b * seq, d), o_b.reshape(b * seq, d), z,
                               hg_gnorm[a][None, :], hg_w_out[a].astype(BF16), x2, gt_m, seq)
        else:
            bi = l // N_MIXERS
            y = _nmm_call(x2, g_mix_l, sh_m, sc_m, cv_w_pw1[bi].astype(BF16), seq, F32,
                          glu_bias=cv_b_pw1[bi][None, :])
            y = _conv_call(y.reshape(b, seq, d), cv_w_dw[bi])
            x2 = _ln_out_call(y.reshape(b * seq, d), cv_b_dw[bi][None, :], cv_ln_g[bi][None, :],
                              cv_ln_b[bi][None, :], cv_w_pw2[bi].astype(BF16),
                              cv_b_pw2[bi][None, :], x2, gt_m, seq)
        x2 = _peer_layer(x2, g_ffn_l, sh_f, sc_f, gt_f, peer_w_q[l], peer_keys[l],
                         peer_u[l], peer_v[l], seq)
    return _final_call(x2, g_final[None, :]).reshape(b, seq, d)
```

```python
import functools
import math

import numpy as np
import jax
import jax.numpy as jnp
from jax import lax
from jax.experimental import pallas as pl
from jax.experimental.pallas import tpu as pltpu

EPS = 1e-6
GRID_W = 64
CONV_WIDTH = 31
HG_HEADS = 8
PEER_HEADS = 8
PEER_NKEYS = 128
PEER_DKEY = 256
PEER_TOPK = 16
N_MIXERS = 2

LANES = 128
SUBLANES = 8
SCAN_CHUNK = 128
SCAN_MATMUL_LEVELS = int(math.log2(SUBLANES))
PEER_EB = 512
PEER_SUB = 256
MXU_WIDTH = 256
VMEM_LIMIT = 56 * 1024 * 1024

F32 = jnp.float32
BF16 = jnp.bfloat16


def _tile(n, pref):
    t = min(n, pref)
    while n % t:
        t //= 2
    return t


def _params(sem):
    return pltpu.CompilerParams(dimension_semantics=sem, vmem_limit_bytes=VMEM_LIMIT)


def _rmsnorm_mod(x, g, sh, sc):
    y = x * lax.rsqrt(jnp.mean(x * x, axis=-1, keepdims=True) + EPS) * g
    return y * (1.0 + sc) + sh


def _ada_kernel(c_ref, w_ref, b_ref, o_ref):
    s = c_ref[...]
    s = s * jax.nn.sigmoid(s)
    o_ref[...] = jnp.dot(s, w_ref[...], preferred_element_type=F32,
                         precision=lax.Precision.HIGHEST) + b_ref[...]


def _ada_call(cc, w_ada, b_ada):
    depth, d, n = w_ada.shape
    r = cc.shape[0]
    tn = _tile(n, 1024)
    return pl.pallas_call(
        _ada_kernel,
        grid=(depth, n // tn),
        in_specs=[pl.BlockSpec((r, d), lambda l, j: (0, 0)),
                  pl.BlockSpec((None, d, tn), lambda l, j: (l, 0, j)),
                  pl.BlockSpec((None, 1, tn), lambda l, j: (l, 0, j))],
        out_specs=pl.BlockSpec((None, r, tn), lambda l, j: (l, 0, j)),
        out_shape=jax.ShapeDtypeStruct((depth, r, n), F32),
        compiler_params=_params(("parallel", "parallel")),
        name="ada",
    )(cc, w_ada, b_ada.reshape(depth, 1, n))


def _nmm_kernel(x_ref, g_ref, sh_ref, sc_ref, w_ref, o_ref, h_scr):
    @pl.when(pl.program_id(1) == 0)
    def _():
        h_scr[...] = _rmsnorm_mod(x_ref[...], g_ref[...], sh_ref[...], sc_ref[...]).astype(BF16)

    o_ref[...] = jnp.dot(h_scr[...], w_ref[...], preferred_element_type=F32).astype(o_ref.dtype)


def _nmm_glu_kernel(x_ref, g_ref, sh_ref, sc_ref, wa_ref, wg_ref, ba_ref, bg_ref, o_ref, h_scr):
    @pl.when(pl.program_id(1) == 0)
    def _():
        h_scr[...] = _rmsnorm_mod(x_ref[...], g_ref[...], sh_ref[...], sc_ref[...]).astype(BF16)

    h = h_scr[...]
    a = jnp.dot(h, wa_ref[...], preferred_element_type=F32) + ba_ref[...]
    g = jnp.dot(h, wg_ref[...], preferred_element_type=F32) + bg_ref[...]
    o_ref[...] = (a * jax.nn.sigmoid(g)).astype(o_ref.dtype)


def _nmm_call(x2, g, sh, sc, w, rows_per_mod, out_dtype, glu_bias=None):
    m, d = x2.shape
    n = w.shape[1]
    tm = _tile(rows_per_mod, 1024)
    per = rows_per_mod // tm
    x_spec = pl.BlockSpec((tm, d), lambda i, j: (i, 0))
    g_spec = pl.BlockSpec((1, d), lambda i, j: (0, 0))
    mod_spec = pl.BlockSpec((None, 1, d), lambda i, j: (i // per, 0, 0))
    scratch = [pltpu.VMEM((tm, d), BF16)]
    if glu_bias is None:
        tn = _tile(n, 1024)
        return pl.pallas_call(
            _nmm_kernel,
            grid=(m // tm, n // tn),
            in_specs=[x_spec, g_spec, mod_spec, mod_spec,
                      pl.BlockSpec((d, tn), lambda i, j: (0, j))],
            out_specs=pl.BlockSpec((tm, tn), lambda i, j: (i, j)),
            out_shape=jax.ShapeDtypeStruct((m, n), out_dtype),
            scratch_shapes=scratch,
            compiler_params=_params(("parallel", "arbitrary")),
            name="norm_mod_matmul",
        )(x2, g, sh, sc, w)
    nh = n // 2
    tn = _tile(nh, 512)
    nb = nh // tn
    return pl.pallas_call(
        _nmm_glu_kernel,
        grid=(m // tm, nb),
        in_specs=[x_spec, g_spec, mod_spec, mod_spec,
                  pl.BlockSpec((d, tn), lambda i, j: (0, j)),
                  pl.BlockSpec((d, tn), lambda i, j: (0, j + nb)),
                  pl.BlockSpec((1, tn), lambda i, j: (0, j)),
                  pl.BlockSpec((1, tn), lambda i, j: (0, j + nb))],
        out_specs=pl.BlockSpec((tm, tn), lambda i, j: (i, j)),
        out_shape=jax.ShapeDtypeStruct((m, nh), out_dtype),
        scratch_shapes=scratch,
        compiler_params=_params(("parallel", "arbitrary")),
        name="norm_mod_matmul_glu",
    )(x2, g, sh, sc, w, w, glu_bias, glu_bias)


@functools.lru_cache(maxsize=None)
def _scan_consts(c):
    nl = int(math.log2(c))
    assert 1 << nl == c
    idx = np.arange(c)
    mats = np.zeros((2, 2 + SCAN_MATMUL_LEVELS, c, c), np.float32)
    level = np.zeros((2, c, c), np.int32)
    for d in range(2):
        tau = idx if d == 0 else c - 1 - idx
        tt, tr = tau[:, None], tau[None, :]
        mats[d, 0] = tr <= tt
        mats[d, 1] = tr > tt
        for l in range(SCAN_MATMUL_LEVELS):
            half = 1 << l
            mid = (tau // (2 * half)) * (2 * half) + half - 1
            later = ((tau >> l) & 1) == 1
            mats[d, 2 + l] = ((later[:, None] & (tr > mid[:, None]) & (tr <= tt))
                              | ((~later)[:, None] & (tr > tt) & (tr <= mid[:, None])))
        xor = tt ^ tr
        lev = np.floor(np.log2(np.maximum(xor, 1))).astype(np.int32)
        level[d] = np.where(tt > tr, lev, np.where(tt == tr, -1, -2))
    return mats.reshape(2, (2 + SCAN_MATMUL_LEVELS) * c, c), level, nl


def _scan_direction(d, zf, v, q, lb, est_ref, lev_ref, st_ref, o_ref, n_levels):
    c = zf.shape[0]
    f = lb + (1.0 - lb) * jax.nn.sigmoid(zf)
    logf = jnp.log(f)
    kk = (1.0 - lb) * jax.nn.sigmoid(-zf)
    lev = lev_ref[d]

    def block_diag(x):
        zero = jnp.zeros((x.shape[0], LANES), BF16)
        return jnp.concatenate([jnp.concatenate([x[:, :LANES], zero], axis=1),
                                jnp.concatenate([zero, x[:, LANES:]], axis=1)], axis=0)

    nt = (((1,), (1,)), ((), ()))
    for hp in range(st_ref.shape[0]):
        sl = slice(2 * hp * LANES, 2 * (hp + 1) * LANES)
        lf = logf[:, sl]
        hi = lf.astype(BF16)
        mid = (lf - hi.astype(F32)).astype(BF16)
        ex = jnp.dot(est_ref[d], jnp.concatenate([hi, mid], axis=0), preferred_element_type=F32)
        p = jnp.exp(ex)
        dec = jnp.exp(ex[0:1, :] + ex[c:c + 1, :])
        q_h, k_h, v_h = q[:, sl], kk[:, sl], v[:, sl]
        st = st_ref[hp]
        qb = (q_h * p[0:c]).astype(BF16)
        inter = lax.dot_general(qb, block_diag(st.astype(BF16)), nt, preferred_element_type=F32)
        pr = lax.dot_general(q_h.astype(BF16), block_diag(k_h.astype(BF16)), nt,
                             preferred_element_type=F32)
        scores = jnp.where(lev == -1, pr, 0.0)
        b = ex[0:c]
        for l in range(n_levels):
            if l < SCAN_MATMUL_LEVELS:
                pl_ = p[(2 + l) * c:(3 + l) * c]
            else:
                half = 1 << l
                rows = [jnp.broadcast_to(b[p0 + half - 1 + d:p0 + half + d, :], (2 * half, 2 * LANES))
                        for p0 in range(0, c, 2 * half)]
                bmid = jnp.concatenate(rows, axis=0) if len(rows) > 1 else rows[0]
                pl_ = jnp.exp(-jnp.abs(b - bmid))
            ql = (q_h * pl_).astype(BF16)
            kl = (k_h * pl_).astype(BF16)
            pr = lax.dot_general(ql, block_diag(kl), nt, preferred_element_type=F32)
            scores = jnp.where(lev == l, pr, scores)
        intra = jnp.dot(scores.astype(BF16), block_diag(v_h.astype(BF16)), preferred_element_type=F32)
        o_ref[:, sl] = inter + intra
        kend = (k_h * p[c:2 * c]).astype(BF16)
        vt = jnp.concatenate([v_h[:, :LANES].T, v_h[:, LANES:].T], axis=1).astype(BF16)
        st_ref[hp] = st * dec + jnp.dot(vt, block_diag(kend), preferred_element_type=F32)


def _scan_kernel(lbraw_ref, est_ref, lev_ref, s0f_ref, s0b_ref,
                 vf_ref, ff_ref, qf_ref, vb_ref, fb_ref, qb_ref,
                 of_ref, ob_ref, sf_ref, sb_ref, stf_scr, stb_scr, *, layer, n_levels):
    j = pl.program_id(1)

    @pl.when(j == 0)
    def _():
        stf_scr[...] = s0f_ref[...]
        stb_scr[...] = s0b_ref[...]

    raw = lbraw_ref[...]
    e = jnp.exp(raw - jnp.max(raw, axis=0, keepdims=True))
    lb_all = jnp.sum(e[0:layer + 1], axis=0, keepdims=True) / jnp.sum(e, axis=0, keepdims=True)
    d_model = vf_ref.shape[-1]
    _scan_direction(0, ff_ref[...], vf_ref[...], qf_ref[...], lb_all[:, :d_model],
                    est_ref, lev_ref, stf_scr, of_ref, n_levels)
    _scan_direction(1, fb_ref[...], vb_ref[...], qb_ref[...], lb_all[:, d_model:],
                    est_ref, lev_ref, stb_scr, ob_ref, n_levels)

    @pl.when(j == pl.num_programs(1) - 1)
    def _():
        sf_ref[...] = stf_scr[...]
        sb_ref[...] = stb_scr[...]


def _scan_call(z, hg_lb, s0f, s0b, layer):
    b, l, d5 = z.shape
    d = d5 // 5
    pairs = d // (2 * LANES)
    c = SCAN_CHUNK
    nc = l // c
    est, lev, n_levels = _scan_consts(c)
    est = jnp.asarray(np.concatenate([est, est], axis=2), BF16)
    lev = jnp.asarray(np.concatenate([lev, lev], axis=2))

    def zspec(col, rev):
        if rev:
            return pl.BlockSpec((None, c, d), lambda bi, j: (bi, nc - 1 - j, col))
        return pl.BlockSpec((None, c, d), lambda bi, j: (bi, j, col))

    full = lambda a: pl.BlockSpec(a.shape, lambda bi, j: (0,) * a.ndim)
    st_spec = pl.BlockSpec((None, pairs, LANES, 2 * LANES), lambda bi, j: (bi, 0, 0, 0))
    return pl.pallas_call(
        functools.partial(_scan_kernel, layer=layer, n_levels=n_levels),
        grid=(b, nc),
        in_specs=[full(hg_lb), full(est), full(lev), st_spec, st_spec,
                  zspec(0, False), zspec(1, False), zspec(3, False),
                  zspec(0, True), zspec(2, True), zspec(3, True)],
        out_specs=[pl.BlockSpec((None, c, d), lambda bi, j: (bi, j, 0)),
                   pl.BlockSpec((None, c, d), lambda bi, j: (bi, nc - 1 - j, 0)),
                   st_spec, st_spec],
        out_shape=[jax.ShapeDtypeStruct((b, l, d), F32), jax.ShapeDtypeStruct((b, l, d), F32),
                   jax.ShapeDtypeStruct(s0f.shape, F32), jax.ShapeDtypeStruct(s0b.shape, F32)],
        scratch_shapes=[pltpu.VMEM((pairs, LANES, 2 * LANES), F32),
                        pltpu.VMEM((pairs, LANES, 2 * LANES), F32)],
        compiler_params=_params(("parallel", "arbitrary")),
        name="hgrn2_scan",
    )(hg_lb, est, lev, s0f, s0b, z, z, z, z, z, z)


def _mix_out_kernel(of_ref, ob_ref, gate_ref, gn_ref, w_ref, x_ref, gt_ref, o_ref, h_scr):
    @pl.when(pl.program_id(1) == 0)
    def _():
        o = of_ref[...] + ob_ref[...]
        gate = gate_ref[...]
        gn = gn_ref[...]
        for h in range(o.shape[1] // LANES):
            sl = slice(h * LANES, (h + 1) * LANES)
            oh = o[:, sl]
            y = oh * lax.rsqrt(jnp.mean(oh * oh, axis=-1, keepdims=True) + EPS) * gn
            gh = gate[:, sl]
            h_scr[:, sl] = (y * (gh * jax.nn.sigmoid(gh))).astype(BF16)

    mix = jnp.dot(h_scr[...], w_ref[...], preferred_element_type=F32)
    o_ref[...] = x_ref[...] + gt_ref[...] * mix


def _mix_out_call(o_f, o_b, z2, gnorm, w_out, x2, gt, rows_per_mod):
    m, d = x2.shape
    tm = _tile(rows_per_mod, 1024)
    per = rows_per_mod // tm
    tn = _tile(d, 512)
    row = pl.BlockSpec((tm, d), lambda i, j: (i, 0))
    return pl.pallas_call(
        _mix_out_kernel,
        grid=(m // tm, d // tn),
        in_specs=[row, row,
                  pl.BlockSpec((tm, d), lambda i, j: (i, 4)),
                  pl.BlockSpec((1, LANES), lambda i, j: (0, 0)),
                  pl.BlockSpec((d, tn), lambda i, j: (0, j)),
                  pl.BlockSpec((tm, tn), lambda i, j: (i, j)),
                  pl.BlockSpec((None, 1, tn), lambda i, j: (i // per, 0, j))],
        out_specs=pl.BlockSpec((tm, tn), lambda i, j: (i, j)),
        out_shape=jax.ShapeDtypeStruct((m, d), F32),
        scratch_shapes=[pltpu.VMEM((tm, d), BF16)],
        compiler_params=_params(("parallel", "arbitrary")),
        name="hgrn2_out",
    )(o_f, o_b, z2, gnorm, w_out, x2, gt)


def _conv_kernel(y_ref, w_ref, o_ref, padw_scr, padh_scr, *, rows, n_w_tiles):
    ct = pl.program_id(1)
    half = (CONV_WIDTH - 1) // 2
    l, lanes = y_ref.shape
    lead = padw_scr.shape[1] - GRID_W - 16

    @pl.when(ct < n_w_tiles)
    def _():
        padw_scr[...] = jnp.zeros(padw_scr.shape, F32)
        padw_scr[:, lead:lead + GRID_W, :] = y_ref[...].reshape(rows, GRID_W, lanes)
        acc = jnp.zeros((rows, GRID_W, lanes), F32)
        for k in range(CONV_WIDTH):
            start = lead + k - half
            acc = acc + w_ref[k:k + 1, :].reshape(1, 1, lanes) * padw_scr[:, start:start + GRID_W, :]
        o_ref[...] = acc.reshape(l, lanes)

    @pl.when(ct >= n_w_tiles)
    def _():
        margin = half * GRID_W
        padh_scr[...] = jnp.zeros(padh_scr.shape, F32)
        padh_scr[margin:margin + l, :] = y_ref[...]
        acc = jnp.zeros((l, lanes), F32)
        for k in range(CONV_WIDTH):
            acc = acc + w_ref[k:k + 1, :] * padh_scr[k * GRID_W:k * GRID_W + l, :]
        o_ref[...] = acc


def _conv_call(y, w_dw):
    b, l, c = y.shape
    rows = l // GRID_W
    half = (CONV_WIDTH - 1) // 2
    n_w_tiles = (c // 2) // LANES
    wp = jnp.pad(w_dw, ((0, 32 - CONV_WIDTH), (0, 0)))
    return pl.pallas_call(
        functools.partial(_conv_kernel, rows=rows, n_w_tiles=n_w_tiles),
        grid=(b, c // LANES),
        in_specs=[pl.BlockSpec((None, l, LANES), lambda bi, ct: (bi, 0, ct)),
                  pl.BlockSpec((32, LANES), lambda bi, ct: (0, ct))],
        out_specs=pl.BlockSpec((None, l, LANES), lambda bi, ct: (bi, 0, ct)),
        out_shape=jax.ShapeDtypeStruct((b, l, c), F32),
        scratch_shapes=[pltpu.VMEM((rows, GRID_W + 32, LANES), F32),
                        pltpu.VMEM((l + 2 * half * GRID_W, LANES), F32)],
        compiler_params=_params(("parallel", "parallel")),
        name="axial_dwconv",
    )(y, wp)


def _ln_out_kernel(y_ref, bdw_ref, lg_ref, lb_ref, w_ref, b2_ref, x_ref, gt_ref, o_ref, h_scr):
    @pl.when(pl.program_id(1) == 0)
    def _():
        y = y_ref[...] + bdw_ref[...]
        yc = y - jnp.mean(y, axis=-1, keepdims=True)
        yn = yc * lax.rsqrt(jnp.mean(yc * yc, axis=-1, keepdims=True) + EPS)
        yn = yn * lg_ref[...] + lb_ref[...]
        h_scr[...] = (yn * jax.nn.sigmoid(yn)).astype(BF16)

    mix = jnp.dot(h_scr[...], w_ref[...], preferred_element_type=F32) + b2_ref[...]
    o_ref[...] = x_ref[...] + gt_ref[...] * mix


def _ln_out_call(y2, b_dw, ln_g, ln_b, w_pw2, b_pw2, x2, gt, rows_per_mod):
    m, d = x2.shape
    tm = _tile(rows_per_mod, 1024)
    per = rows_per_mod // tm
    tn = _tile(d, 512)
    vec = pl.BlockSpec((1, d), lambda i, j: (0, 0))
    return pl.pallas_call(
        _ln_out_kernel,
        grid=(m // tm, d // tn),
        in_specs=[pl.BlockSpec((tm, d), lambda i, j: (i, 0)), vec, vec, vec,
                  pl.BlockSpec((d, tn), lambda i, j: (0, j)),
                  pl.BlockSpec((1, tn), lambda i, j: (0, j)),
                  pl.BlockSpec((tm, tn), lambda i, j: (i, j)),
                  pl.BlockSpec((None, 1, tn), lambda i, j: (i // per, 0, j))],
        out_specs=pl.BlockSpec((tm, tn), lambda i, j: (i, j)),
        out_shape=jax.ShapeDtypeStruct((m, d), F32),
        scratch_shapes=[pltpu.VMEM((tm, d), BF16)],
        compiler_params=_params(("parallel", "arbitrary")),
        name="conv_out",
    )(y2, b_dw, ln_g, ln_b, w_pw2, b_pw2, x2, gt)


def _vmax(a, b):
    if a is None:
        return b
    if b is None:
        return a
    return jnp.maximum(a, b)


def _vmin(a, b):
    if a is None or b is None:
        return None
    return jnp.minimum(a, b)


def _bitonic_merge_desc(xs):
    n = len(xs)
    j = n // 2
    while j >= 1:
        for i in range(n):
            o = i ^ j
            if o > i:
                a, b = xs[i], xs[o]
                xs[i], xs[o] = _vmax(a, b), _vmin(a, b)
        j //= 2
    return xs


def _bitonic_sort_desc(xs):
    n = len(xs)
    k = 2
    while k <= n:
        j = k // 2
        while j >= 1:
            for i in range(n):
                o = i ^ j
                if o > i:
                    a, b = xs[i], xs[o]
                    if (i & k) == 0:
                        xs[i], xs[o] = _vmax(a, b), _vmin(a, b)
                    else:
                        xs[i], xs[o] = _vmin(a, b), _vmax(a, b)
            j //= 2
        k *= 2
    return xs


def _merge_top(a, b):
    n = len(a)
    b = list(b) + [None] * (n - len(b))
    return _bitonic_merge_desc([_vmax(a[i], b[n - 1 - i]) for i in range(n)])


def _topk_kernel(q_ref, keys_ref, th_ref, e1_ref, s2_ref, e2_ref, s_scr):
    heads = th_ref.shape[0]
    nk = keys_ref.shape[1]
    k = PEER_TOPK
    t = q_ref.shape[0]
    sub = lax.broadcasted_iota(jnp.int32, (SUBLANES, t), 0)
    tops = [[None] * k, [None] * k]
    for h in range(heads):
        for p in range(2):
            hp = 2 * h + p
            dk = keys_ref.shape[2]
            s = lax.dot_general(keys_ref[hp], q_ref[:, hp * dk:(hp + 1) * dk],
                                (((1,), (1,)), ((), ())), preferred_element_type=F32)
            s_scr[hp] = s
            groups = _bitonic_sort_desc([s[SUBLANES * a:SUBLANES * (a + 1), :] for a in range(nk // SUBLANES)])
            assert len(groups) == k
            for shift in (4, 2, 1):
                groups = _merge_top(groups, [pltpu.roll(g, shift, 0) for g in groups])
            for a in range(k):
                prev = tops[p][a]
                tops[p][a] = groups[a] if prev is None else jnp.where(sub == h, groups[a], prev)
    v1, v2 = tops
    best = [v1[0] + v2[b] for b in range(k)]
    for a in range(1, k // 2):
        best = _merge_top(best, [v1[a] + v2[b] for b in range(k // (a + 1))])
    best = _merge_top(best, [v1[a] + v2[0] for a in range(k // 2, k)])
    cmax = best[0]
    zsum = jnp.ones_like(cmax)
    for a in range(1, k):
        zsum = zsum + jnp.exp(best[a] - cmax)
    tau = best[k - 1]
    inv_z = 1.0 / zsum
    for h in range(heads):
        s1 = s_scr[2 * h]
        s2 = s_scr[2 * h + 1]
        tau_h = tau[h:h + 1, :]
        slack = (jnp.abs(tau_h) + jnp.abs(s1)) * (2.0 ** -22)
        th_ref[h] = (tau_h - s1) - slack
        e1_ref[h] = jnp.exp(s1 - v1[0][h:h + 1, :])
        s2_ref[h] = s2
        e2_ref[h] = jnp.exp(s2 - v2[0][h:h + 1, :]) * inv_z[h:h + 1, :]


def _topk_call(qk, keys):
    m = qk.shape[0]
    hp, nk, dk = keys.shape
    heads = hp // 2
    t = _tile(m, 512)
    out = jax.ShapeDtypeStruct((heads, nk, m), F32)
    ospec = pl.BlockSpec((heads, nk, t), lambda i: (0, 0, i))
    return pl.pallas_call(
        _topk_kernel,
        grid=(m // t,),
        in_specs=[pl.BlockSpec((t, hp * dk), lambda i: (i, 0)),
                  pl.BlockSpec((hp, nk, dk), lambda i: (0, 0, 0))],
        out_specs=[ospec, ospec, ospec, ospec],
        out_shape=[out, out, out, out],
        scratch_shapes=[pltpu.VMEM((hp, nk, t), F32)],
        compiler_params=_params(("parallel",)),
        name="peer_topk",
    )(qk, keys)


def _peer_weights(i, a, g_out, row0, bc_scr, th_ref, e1_ref, s2_ref, e2_ref):
    heads, nk, t = th_ref.shape
    for h in range(heads):
        bc_scr[0, h] = jnp.broadcast_to(th_ref[h, pl.ds(i, 1), :], (SUBLANES, t))
        bc_scr[1, h] = jnp.broadcast_to(e1_ref[h, pl.ds(i, 1), :], (SUBLANES, t))
    for tl in range(t // LANES):
        lanes = slice(tl * LANES, (tl + 1) * LANES)
        w = None
        for h in range(heads):
            th = jnp.concatenate([bc_scr[0, h, :, lanes]] * (nk // SUBLANES), axis=0)
            e1 = jnp.concatenate([bc_scr[1, h, :, lanes]] * (nk // SUBLANES), axis=0)
            term = jnp.where(s2_ref[h, :, lanes] >= th, e2_ref[h, :, lanes], 0.0) * e1
            w = term if w is None else w + term
        a_t = a[:, lanes]
        act = 0.5 * a_t * (1.0 + lax.erf(a_t * (1.0 / math.sqrt(2.0))))
        g_out[row0:row0 + nk, lanes] = (w * act).astype(BF16)


def _peer_kernel(x_ref, g_ref, sh_ref, sc_ref, gt_ref, th_ref, e1_ref, s2_ref, e2_ref,
                 u_ref, vtp_ref, vtc_ref, vtl_ref, o_ref, ht_scr, ga_scr, gb_scr, acc_scr, bc_scr):
    kb = pl.program_id(1)
    nk = th_ref.shape[1]
    n_sub = PEER_EB // PEER_SUB
    sel = (th_ref, e1_ref, s2_ref, e2_ref)

    @pl.when(kb == 0)
    def _():
        h = _rmsnorm_mod(x_ref[...], g_ref[...], sh_ref[...], sc_ref[...])
        ht_scr[...] = h.T.astype(BF16)
        gb_scr[...] = jnp.zeros(gb_scr.shape, BF16)
        acc_scr[...] = jnp.zeros(acc_scr.shape, F32)

    def first_matmul(idx):
        rows = slice(idx * PEER_SUB, (idx + 1) * PEER_SUB)
        return jnp.dot(u_ref[rows, :], ht_scr[...], preferred_element_type=F32)

    a_next = first_matmul(0)
    for idx in range(2 * n_sub):
        phase, sb = divmod(idx, n_sub)
        a = a_next
        if idx + 1 < 2 * n_sub:
            a_next = first_matmul(idx + 1)
        if sb == min(1, n_sub - 1):
            vt_ref, g_in = (vtp_ref, gb_scr) if phase == 0 else (vtc_ref, ga_scr)
            acc_scr[...] += jnp.dot(vt_ref[...], g_in[...], preferred_element_type=F32)
        g_out = ga_scr if phase == 0 else gb_scr
        for gi in range(PEER_SUB // nk):
            i = kb * (2 * PEER_EB // nk) + idx * (PEER_SUB // nk) + gi
            _peer_weights(i, a[gi * nk:(gi + 1) * nk, :], g_out, sb * PEER_SUB + gi * nk, bc_scr, *sel)

    @pl.when(kb == pl.num_programs(1) - 1)
    def _():
        acc = acc_scr[...] + jnp.dot(vtl_ref[...], gb_scr[...], preferred_element_type=F32)
        o_ref[...] = x_ref[...] + gt_ref[...] * acc.T


def _peer_call(x2, g, sh, sc, gt, th, e1, s2, e2, u, vt, rows_per_mod):
    m, d = x2.shape
    heads, nk, _ = th.shape
    e = u.shape[0]
    t = _tile(rows_per_mod, 512)
    per = rows_per_mod // t
    eb = PEER_EB
    nkb = e // (2 * eb)
    assert nkb * 2 * eb == e
    sel = pl.BlockSpec((heads, nk, t), lambda i, k: (0, 0, i))
    mod = pl.BlockSpec((None, 1, d), lambda i, k: (i // per, 0, 0))
    return pl.pallas_call(
        _peer_kernel,
        grid=(m // t, nkb),
        in_specs=[pl.BlockSpec((t, d), lambda i, k: (i, 0)),
                  pl.BlockSpec((1, d), lambda i, k: (0, 0)),
                  mod, mod, mod, sel, sel, sel, sel,
                  pl.BlockSpec((2 * eb, d), lambda i, k: (k, 0)),
                  pl.BlockSpec((d, eb), lambda i, k: (0, jnp.maximum(2 * k - 1, 0))),
                  pl.BlockSpec((d, eb), lambda i, k: (0, 2 * k)),
                  pl.BlockSpec((d, eb), lambda i, k: (0, 2 * nkb - 1))],
        out_specs=pl.BlockSpec((t, d), lambda i, k: (i, 0)),
        out_shape=jax.ShapeDtypeStruct((m, d), F32),
        scratch_shapes=[pltpu.VMEM((d, t), BF16), pltpu.VMEM((eb, t), BF16),
                        pltpu.VMEM((eb, t), BF16), pltpu.VMEM((d, t), F32),
                        pltpu.VMEM((2, heads, SUBLANES, t), F32)],
        compiler_params=_params(("parallel", "arbitrary")),
        name="peer_experts",
    )(x2, g, sh, sc, gt, th, e1, s2, e2, u, vt, vt, vt)


def _final_kernel(x_ref, g_ref, o_ref):
    x = x_ref[...]
    o_ref[...] = x * lax.rsqrt(jnp.mean(x * x, axis=-1, keepdims=True) + EPS) * g_ref[...]


def _final_call(x2, g):
    m, d = x2.shape
    tm = _tile(m, 1024)
    return pl.pallas_call(
        _final_kernel,
        grid=(m // tm,),
        in_specs=[pl.BlockSpec((tm, d), lambda i: (i, 0)), pl.BlockSpec((1, d), lambda i: (0, 0))],
        out_specs=pl.BlockSpec((tm, d), lambda i: (i, 0)),
        out_shape=jax.ShapeDtypeStruct((m, d), F32),
        compiler_params=_params(("parallel",)),
        name="final_norm",
    )(x2, g)


def _peer_layer(x2, g_ffn, sh, sc, gt, w_q, keys, u, v, seq):
    d = x2.shape[1]
    qk = _nmm_call(x2, g_ffn, sh, sc, w_q.astype(BF16), seq, BF16)
    kb = keys.reshape(2 * PEER_HEADS, PEER_NKEYS, PEER_DKEY // 2).astype(BF16)
    th, e1, s2, e2 = _topk_call(qk, kb)
    return _peer_call(x2, g_ffn, sh, sc, gt, th, e1, s2, e2,
                      u.astype(BF16), v.astype(BF16).T, seq)


def kernel(x, c, ctx, c_ctx, w_ada, b_ada, g_mix, g_ffn, hg_w_in, hg_w_out, hg_gnorm, hg_lb,
           cv_w_pw1, cv_b_pw1, cv_w_dw, cv_b_dw, cv_ln_g, cv_ln_b, cv_w_pw2, cv_b_pw2,
           peer_w_q, peer_keys, peer_u, peer_v, g_final):
    b, seq, d = x.shape
    ctx_len = ctx.shape[1]
    depth = w_ada.shape[0]
    heads = d // LANES
    assert heads * LANES == d and seq % SCAN_CHUNK == 0 and ctx_len % SCAN_CHUNK == 0

    pad_rows = (-(b + 1)) % SUBLANES
    cc = jnp.concatenate([c, c_ctx[None, :], jnp.zeros((pad_rows, d), F32)], axis=0)
    ada = _ada_call(cc, w_ada, b_ada)

    x2 = x.reshape(b * seq, d)
    for l in range(depth):
        mixer = l % N_MIXERS
        mods = ada[l, :b].reshape(b, 1, 6, d)
        sh_m, sc_m, gt_m, sh_f, sc_f, gt_f = [mods[:, :, i, :] for i in range(6)]
        g_mix_l = g_mix[l][None, :]
        g_ffn_l = g_ffn[l][None, :]
        if mixer == 0:
            a = l // N_MIXERS
            w_in = hg_w_in[a].astype(BF16)
            cmods = ada[l, b].reshape(6, d)
            csh = jnp.broadcast_to(cmods[0][None, None, :], (b, 1, d))
            csc = jnp.broadcast_to(cmods[1][None, None, :], (b, 1, d))
            zc = _nmm_call(ctx.reshape(b * ctx_len, d), g_mix_l, csh, csc, w_in, ctx_len, F32)
            s0 = jnp.zeros((b, heads // 2, LANES, 2 * LANES), F32)
            _, _, s_f, s_b = _scan_call(zc.reshape(b, ctx_len, 5 * d), hg_lb, s0, s0, l)
            z = _nmm_call(x2, g_mix_l, sh_m, sc_m, w_in, seq, F32)
            o_f, o_b, _, _ = _scan_call(z.reshape(b, seq, 5 * d), hg_lb, s_f, s_b, l)
            x2 = _mix_out_call(o_f.reshape(b * seq, d), o_b.reshape(b * seq, d), z,
                               hg_gnorm[a][None, :], hg_w_out[a].astype(BF16), x2, gt_m, seq)
        else:
            bi = l // N_MIXERS
            y = _nmm_call(x2, g_mix_l, sh_m, sc_m, cv_w_pw1[bi].astype(BF16), seq, F32,
                          glu_bias=cv_b_pw1[bi][None, :])
            y = _conv_call(y.reshape(b, seq, d), cv_w_dw[bi])
            x2 = _ln_out_call(y.reshape(b * seq, d), cv_b_dw[bi][None, :], cv_ln_g[bi][None, :],
                              cv_ln_b[bi][None, :], cv_w_pw2[bi].astype(BF16),
                              cv_b_pw2[bi][None, :], x2, gt_m, seq)
        x2 = _peer_layer(x2, g_ffn_l, sh_f, sc_f, gt_f, peer_w_q[l], peer_keys[l],
                         peer_u[l], peer_v[l], seq)
    return _final_call(x2, g_final[None, :]).reshape(b, seq, d)
```

```python
import functools
import math

import numpy as np
import jax
import jax.numpy as jnp
from jax import lax
from jax.experimental import pallas as pl
from jax.experimental.pallas import tpu as pltpu

EPS = 1e-6
GRID_W = 64
CONV_WIDTH = 31
HG_HEADS = 8
PEER_HEADS = 8
PEER_NKEYS = 128
PEER_DKEY = 256
PEER_TOPK = 16
N_MIXERS = 2

LANES = 128
SUBLANES = 8
SCAN_CHUNK = 128
SCAN_MATMUL_LEVELS = int(math.log2(SUBLANES))
PEER_EB = 512
PEER_SUB = 256
MXU_WIDTH = 256
VMEM_LIMIT = 56 * 1024 * 1024

F32 = jnp.float32
BF16 = jnp.bfloat16


def _tile(n, pref):
    t = min(n, pref)
    while n % t:
        t //= 2
    return t


def _params(sem):
    return pltpu.CompilerParams(dimension_semantics=sem, vmem_limit_bytes=VMEM_LIMIT)


def _rmsnorm_mod(x, g, sh, sc):
    y = x * lax.rsqrt(jnp.mean(x * x, axis=-1, keepdims=True) + EPS) * g
    return y * (1.0 + sc) + sh


def _ada_kernel(c_ref, w_ref, b_ref, o_ref):
    s = c_ref[...]
    s = s * jax.nn.sigmoid(s)
    o_ref[...] = jnp.dot(s, w_ref[...], preferred_element_type=F32,
                         precision=lax.Precision.HIGHEST) + b_ref[...]


def _ada_call(cc, w_ada, b_ada):
    depth, d, n = w_ada.shape
    r = cc.shape[0]
    tn = _tile(n, 1024)
    return pl.pallas_call(
        _ada_kernel,
        grid=(depth, n // tn),
        in_specs=[pl.BlockSpec((r, d), lambda l, j: (0, 0)),
                  pl.BlockSpec((None, d, tn), lambda l, j: (l, 0, j)),
                  pl.BlockSpec((None, 1, tn), lambda l, j: (l, 0, j))],
        out_specs=pl.BlockSpec((None, r, tn), lambda l, j: (l, 0, j)),
        out_shape=jax.ShapeDtypeStruct((depth, r, n), F32),
        compiler_params=_params(("parallel", "parallel")),
        name="ada",
    )(cc, w_ada, b_ada.reshape(depth, 1, n))


def _nmm_kernel(x_ref, g_ref, sh_ref, sc_ref, w_ref, o_ref, h_scr):
    @pl.when(pl.program_id(1) == 0)
    def _():
        h_scr[...] = _rmsnorm_mod(x_ref[...], g_ref[...], sh_ref[...], sc_ref[...]).astype(BF16)

    o_ref[...] = jnp.dot(h_scr[...], w_ref[...], preferred_element_type=F32).astype(o_ref.dtype)


def _nmm_glu_kernel(x_ref, g_ref, sh_ref, sc_ref, wa_ref, wg_ref, ba_ref, bg_ref, o_ref, h_scr):
    @pl.when(pl.program_id(1) == 0)
    def _():
        h_scr[...] = _rmsnorm_mod(x_ref[...], g_ref[...], sh_ref[...], sc_ref[...]).astype(BF16)

    h = h_scr[...]
    a = jnp.dot(h, wa_ref[...], preferred_element_type=F32) + ba_ref[...]
    g = jnp.dot(h, wg_ref[...], preferred_element_type=F32) + bg_ref[...]
    o_ref[...] = (a * jax.nn.sigmoid(g)).astype(o_ref.dtype)


def _nmm_call(x2, g, sh, sc, w, rows_per_mod, out_dtype, glu_bias=None):
    m, d = x2.shape
    n = w.shape[1]
    tm = _tile(rows_per_mod, 1024)
    per = rows_per_mod // tm
    x_spec = pl.BlockSpec((tm, d), lambda i, j: (i, 0))
    g_spec = pl.BlockSpec((1, d), lambda i, j: (0, 0))
    mod_spec = pl.BlockSpec((None, 1, d), lambda i, j: (i // per, 0, 0))
    scratch = [pltpu.VMEM((tm, d), BF16)]
    if glu_bias is None:
        tn = _tile(n, 1024)
        return pl.pallas_call(
            _nmm_kernel,
            grid=(m // tm, n // tn),
            in_specs=[x_spec, g_spec, mod_spec, mod_spec,
                      pl.BlockSpec((d, tn), lambda i, j: (0, j))],
            out_specs=pl.BlockSpec((tm, tn), lambda i, j: (i, j)),
            out_shape=jax.ShapeDtypeStruct((m, n), out_dtype),
            scratch_shapes=scratch,
            compiler_params=_params(("parallel", "arbitrary")),
            name="norm_mod_matmul",
        )(x2, g, sh, sc, w)
    nh = n // 2
    tn = _tile(nh, 512)
    nb = nh // tn
    return pl.pallas_call(
        _nmm_glu_kernel,
        grid=(m // tm, nb),
        in_specs=[x_spec, g_spec, mod_spec, mod_spec,
                  pl.BlockSpec((d, tn), lambda i, j: (0, j)),
                  pl.BlockSpec((d, tn), lambda i, j: (0, j + nb)),
                  pl.BlockSpec((1, tn), lambda i, j: (0, j)),
                  pl.BlockSpec((1, tn), lambda i, j: (0, j + nb))],
        out_specs=pl.BlockSpec((tm, tn), lambda i, j: (i, j)),
        out_shape=jax.ShapeDtypeStruct((m, nh), out_dtype),
        scratch_shapes=scratch,
        compiler_params=_params(("parallel", "arbitrary")),
        name="norm_mod_matmul_glu",
    )(x2, g, sh, sc, w, w, glu_bias, glu_bias)


@functools.lru_cache(maxsize=None)
def _scan_consts(c):
    nl = int(math.log2(c))
    assert 1 << nl == c
    idx = np.arange(c)
    mats = np.zeros((2, 2 + SCAN_MATMUL_LEVELS, c, c), np.float32)
    level = np.zeros((2, c, c), np.int32)
    for d in range(2):
        tau = idx if d == 0 else c - 1 - idx
        tt, tr = tau[:, None], tau[None, :]
        mats[d, 0] = tr <= tt
        mats[d, 1] = tr > tt
        for l in range(SCAN_MATMUL_LEVELS):
            half = 1 << l
            mid = (tau // (2 * half)) * (2 * half) + half - 1
            later = ((tau >> l) & 1) == 1
            mats[d, 2 + l] = ((later[:, None] & (tr > mid[:, None]) & (tr <= tt))
                              | ((~later)[:, None] & (tr > tt) & (tr <= mid[:, None])))
        xor = tt ^ tr
        lev = np.floor(np.log2(np.maximum(xor, 1))).astype(np.int32)
        level[d] = np.where(tt > tr, lev, np.where(tt == tr, -1, -2))
    return mats.reshape(2, (2 + SCAN_MATMUL_LEVELS) * c, c), level, nl


def _scan_direction(d, zf, v, q, lb, est_ref, lev_ref, st_ref, o_ref, n_levels):
    c = zf.shape[0]
    f = lb + (1.0 - lb) * jax.nn.sigmoid(zf)
    logf = jnp.log(f) * (1.0 / math.log(2.0))
    kk = (1.0 - lb) * jax.nn.sigmoid(-zf)
    lev = lev_ref[d]

    def block_diag(x):
        zero = jnp.zeros((x.shape[0], LANES), BF16)
        return jnp.concatenate([jnp.concatenate([x[:, :LANES], zero], axis=1),
                                jnp.concatenate([zero, x[:, LANES:]], axis=1)], axis=0)

    nt = (((1,), (1,)), ((), ()))
    for hp in range(st_ref.shape[0]):
        sl = slice(2 * hp * LANES, 2 * (hp + 1) * LANES)
        lf = logf[:, sl]
        hi = lf.astype(BF16)
        mid = (lf - hi.astype(F32)).astype(BF16)
        ex = jnp.dot(est_ref[d], jnp.concatenate([hi, mid], axis=0), preferred_element_type=F32)
        p = jnp.exp2(ex)
        dec = jnp.exp2(ex[0:1, :] + ex[c:c + 1, :])
        q_h, k_h, v_h = q[:, sl], kk[:, sl], v[:, sl]
        st = st_ref[hp]
        qb = (q_h * p[0:c]).astype(BF16)
        inter = lax.dot_general(qb, block_diag(st.astype(BF16)), nt, preferred_element_type=F32)
        pr = lax.dot_general(q_h.astype(BF16), block_diag(k_h.astype(BF16)), nt,
                             preferred_element_type=F32)
        scores = jnp.where(lev == -1, pr, 0.0)
        b = ex[0:c]
        for l in range(n_levels):
            if l < SCAN_MATMUL_LEVELS:
                pl_ = p[(2 + l) * c:(3 + l) * c]
            else:
                half = 1 << l
                rows = [jnp.broadcast_to(b[p0 + half - 1 + d:p0 + half + d, :], (2 * half, 2 * LANES))
                        for p0 in range(0, c, 2 * half)]
                bmid = jnp.concatenate(rows, axis=0) if len(rows) > 1 else rows[0]
                pl_ = jnp.exp2(-jnp.abs(b - bmid))
            ql = (q_h * pl_).astype(BF16)
            kl = (k_h * pl_).astype(BF16)
            pr = lax.dot_general(ql, block_diag(kl), nt, preferred_element_type=F32)
            scores = jnp.where(lev == l, pr, scores)
        intra = jnp.dot(scores.astype(BF16), block_diag(v_h.astype(BF16)), preferred_element_type=F32)
        o_ref[:, sl] = inter + intra
        kend = (k_h * p[c:2 * c]).astype(BF16)
        vt = jnp.concatenate([v_h[:, :LANES].T, v_h[:, LANES:].T], axis=1).astype(BF16)
        st_ref[hp] = st * dec + jnp.dot(vt, block_diag(kend), preferred_element_type=F32)


def _scan_kernel(lbraw_ref, est_ref, lev_ref, s0f_ref, s0b_ref,
                 vf_ref, ff_ref, qf_ref, vb_ref, fb_ref, qb_ref,
                 of_ref, ob_ref, sf_ref, sb_ref, stf_scr, stb_scr, *, layer, n_levels):
    j = pl.program_id(1)

    @pl.when(j == 0)
    def _():
        stf_scr[...] = s0f_ref[...]
        stb_scr[...] = s0b_ref[...]

    raw = lbraw_ref[...]
    e = jnp.exp(raw - jnp.max(raw, axis=0, keepdims=True))
    lb_all = jnp.sum(e[0:layer + 1], axis=0, keepdims=True) / jnp.sum(e, axis=0, keepdims=True)
    d_model = vf_ref.shape[-1]
    _scan_direction(0, ff_ref[...], vf_ref[...], qf_ref[...], lb_all[:, :d_model],
                    est_ref, lev_ref, stf_scr, of_ref, n_levels)
    _scan_direction(1, fb_ref[...], vb_ref[...], qb_ref[...], lb_all[:, d_model:],
                    est_ref, lev_ref, stb_scr, ob_ref, n_levels)

    @pl.when(j == pl.num_programs(1) - 1)
    def _():
        sf_ref[...] = stf_scr[...]
        sb_ref[...] = stb_scr[...]


def _scan_call(z, hg_lb, s0f, s0b, layer):
    b, l, d5 = z.shape
    d = d5 // 5
    pairs = d // (2 * LANES)
    c = SCAN_CHUNK
    nc = l // c
    est, lev, n_levels = _scan_consts(c)
    est = jnp.asarray(np.concatenate([est, est], axis=2), BF16)
    lev = jnp.asarray(np.concatenate([lev, lev], axis=2))

    def zspec(col, rev):
        if rev:
            return pl.BlockSpec((None, c, d), lambda bi, j: (bi, nc - 1 - j, col))
        return pl.BlockSpec((None, c, d), lambda bi, j: (bi, j, col))

    full = lambda a: pl.BlockSpec(a.shape, lambda bi, j: (0,) * a.ndim)
    st_spec = pl.BlockSpec((None, pairs, LANES, 2 * LANES), lambda bi, j: (bi, 0, 0, 0))
    return pl.pallas_call(
        functools.partial(_scan_kernel, layer=layer, n_levels=n_levels),
        grid=(b, nc),
        in_specs=[full(hg_lb), full(est), full(lev), st_spec, st_spec,
                  zspec(0, False), zspec(1, False), zspec(3, False),
                  zspec(0, True), zspec(2, True), zspec(3, True)],
        out_specs=[pl.BlockSpec((None, c, d), lambda bi, j: (bi, j, 0)),
                   pl.BlockSpec((None, c, d), lambda bi, j: (bi, nc - 1 - j, 0)),
                   st_spec, st_spec],
        out_shape=[jax.ShapeDtypeStruct((b, l, d), F32), jax.ShapeDtypeStruct((b, l, d), F32),
                   jax.ShapeDtypeStruct(s0f.shape, F32), jax.ShapeDtypeStruct(s0b.shape, F32)],
        scratch_shapes=[pltpu.VMEM((pairs, LANES, 2 * LANES), F32),
                        pltpu.VMEM((pairs, LANES, 2 * LANES), F32)],
        compiler_params=_params(("parallel", "arbitrary")),
        name="hgrn2_scan",
    )(hg_lb, est, lev, s0f, s0b, z, z, z, z, z, z)


def _mix_out_kernel(of_ref, ob_ref, gate_ref, gn_ref, w_ref, x_ref, gt_ref, o_ref, h_scr):
    @pl.when(pl.program_id(1) == 0)
    def _():
        o = of_ref[...] + ob_ref[...]
        gate = gate_ref[...]
        gn = gn_ref[...]
        for h in range(o.shape[1] // LANES):
            sl = slice(h * LANES, (h + 1) * LANES)
            oh = o[:, sl]
            y = oh * lax.rsqrt(jnp.mean(oh * oh, axis=-1, keepdims=True) + EPS) * gn
            gh = gate[:, sl]
            h_scr[:, sl] = (y * (gh * jax.nn.sigmoid(gh))).astype(BF16)

    mix = jnp.dot(h_scr[...], w_ref[...], preferred_element_type=F32)
    o_ref[...] = x_ref[...] + gt_ref[...] * mix


def _mix_out_call(o_f, o_b, z2, gnorm, w_out, x2, gt, rows_per_mod):
    m, d = x2.shape
    tm = _tile(rows_per_mod, 1024)
    per = rows_per_mod // tm
    tn = _tile(d, 512)
    row = pl.BlockSpec((tm, d), lambda i, j: (i, 0))
    return pl.pallas_call(
        _mix_out_kernel,
        grid=(m // tm, d // tn),
        in_specs=[row, row,
                  pl.BlockSpec((tm, d), lambda i, j: (i, 4)),
                  pl.BlockSpec((1, LANES), lambda i, j: (0, 0)),
                  pl.BlockSpec((d, tn), lambda i, j: (0, j)),
                  pl.BlockSpec((tm, tn), lambda i, j: (i, j)),
                  pl.BlockSpec((None, 1, tn), lambda i, j: (i // per, 0, j))],
        out_specs=pl.BlockSpec((tm, tn), lambda i, j: (i, j)),
        out_shape=jax.ShapeDtypeStruct((m, d), F32),
        scratch_shapes=[pltpu.VMEM((tm, d), BF16)],
        compiler_params=_params(("parallel", "arbitrary")),
        name="hgrn2_out",
    )(o_f, o_b, z2, gnorm, w_out, x2, gt)


def _conv_kernel(y_ref, w_ref, o_ref, padw_scr, padh_scr, *, rows, n_w_tiles):
    ct = pl.program_id(1)
    half = (CONV_WIDTH - 1) // 2
    l, lanes = y_ref.shape
    lead = padw_scr.shape[1] - GRID_W - 16

    @pl.when(ct < n_w_tiles)
    def _():
        padw_scr[...] = jnp.zeros(padw_scr.shape, F32)
        padw_scr[:, lead:lead + GRID_W, :] = y_ref[...].reshape(rows, GRID_W, lanes)
        acc = jnp.zeros((rows, GRID_W, lanes), F32)
        for k in range(CONV_WIDTH):
            start = lead + k - half
            acc = acc + w_ref[k:k + 1, :].reshape(1, 1, lanes) * padw_scr[:, start:start + GRID_W, :]
        o_ref[...] = acc.reshape(l, lanes)

    @pl.when(ct >= n_w_tiles)
    def _():
        margin = half * GRID_W
        padh_scr[...] = jnp.zeros(padh_scr.shape, F32)
        padh_scr[margin:margin + l, :] = y_ref[...]
        acc = jnp.zeros((l, lanes), F32)
        for k in range(CONV_WIDTH):
            acc = acc + w_ref[k:k + 1, :] * padh_scr[k * GRID_W:k * GRID_W + l, :]
        o_ref[...] = acc


def _conv_call(y, w_dw):
    b, l, c = y.shape
    rows = l // GRID_W
    half = (CONV_WIDTH - 1) // 2
    n_w_tiles = (c // 2) // LANES
    wp = jnp.pad(w_dw, ((0, 32 - CONV_WIDTH), (0, 0)))
    return pl.pallas_call(
        functools.partial(_conv_kernel, rows=rows, n_w_tiles=n_w_tiles),
        grid=(b, c // LANES),
        in_specs=[pl.BlockSpec((None, l, LANES), lambda bi, ct: (bi, 0, ct)),
                  pl.BlockSpec((32, LANES), lambda bi, ct: (0, ct))],
        out_specs=pl.BlockSpec((None, l, LANES), lambda bi, ct: (bi, 0, ct)),
        out_shape=jax.ShapeDtypeStruct((b, l, c), F32),
        scratch_shapes=[pltpu.VMEM((rows, GRID_W + 32, LANES), F32),
                        pltpu.VMEM((l + 2 * half * GRID_W, LANES), F32)],
        compiler_params=_params(("parallel", "parallel")),
        name="axial_dwconv",
    )(y, wp)


def _ln_out_kernel(y_ref, bdw_ref, lg_ref, lb_ref, w_ref, b2_ref, x_ref, gt_ref, o_ref, h_scr):
    @pl.when(pl.program_id(1) == 0)
    def _():
        y = y_ref[...] + bdw_ref[...]
        yc = y - jnp.mean(y, axis=-1, keepdims=True)
        yn = yc * lax.rsqrt(jnp.mean(yc * yc, axis=-1, keepdims=True) + EPS)
        yn = yn * lg_ref[...] + lb_ref[...]
        h_scr[...] = (yn * jax.nn.sigmoid(yn)).astype(BF16)

    mix = jnp.dot(h_scr[...], w_ref[...], preferred_element_type=F32) + b2_ref[...]
    o_ref[...] = x_ref[...] + gt_ref[...] * mix


def _ln_out_call(y2, b_dw, ln_g, ln_b, w_pw2, b_pw2, x2, gt, rows_per_mod):
    m, d = x2.shape
    tm = _tile(rows_per_mod, 1024)
    per = rows_per_mod // tm
    tn = _tile(d, 512)
    vec = pl.BlockSpec((1, d), lambda i, j: (0, 0))
    return pl.pallas_call(
        _ln_out_kernel,
        grid=(m // tm, d // tn),
        in_specs=[pl.BlockSpec((tm, d), lambda i, j: (i, 0)), vec, vec, vec,
                  pl.BlockSpec((d, tn), lambda i, j: (0, j)),
                  pl.BlockSpec((1, tn), lambda i, j: (0, j)),
                  pl.BlockSpec((tm, tn), lambda i, j: (i, j)),
                  pl.BlockSpec((None, 1, tn), lambda i, j: (i // per, 0, j))],
        out_specs=pl.BlockSpec((tm, tn), lambda i, j: (i, j)),
        out_shape=jax.ShapeDtypeStruct((m, d), F32),
        scratch_shapes=[pltpu.VMEM((tm, d), BF16)],
        compiler_params=_params(("parallel", "arbitrary")),
        name="conv_out",
    )(y2, b_dw, ln_g, ln_b, w_pw2, b_pw2, x2, gt)


def _vmax(a, b):
    if a is None:
        return b
    if b is None:
        return a
    return jnp.maximum(a, b)


def _vmin(a, b):
    if a is None or b is None:
        return None
    return jnp.minimum(a, b)


def _bitonic_merge_desc(xs):
    n = len(xs)
    j = n // 2
    while j >= 1:
        for i in range(n):
            o = i ^ j
            if o > i:
                a, b = xs[i], xs[o]
                xs[i], xs[o] = _vmax(a, b), _vmin(a, b)
        j //= 2
    return xs


def _bitonic_sort_desc(xs):
    n = len(xs)
    k = 2
    while k <= n:
        j = k // 2
        while j >= 1:
            for i in range(n):
                o = i ^ j
                if o > i:
                    a, b = xs[i], xs[o]
                    if (i & k) == 0:
                        xs[i], xs[o] = _vmax(a, b), _vmin(a, b)
                    else:
                        xs[i], xs[o] = _vmin(a, b), _vmax(a, b)
            j //= 2
        k *= 2
    return xs


def _merge_top(a, b):
    n = len(a)
    b = list(b) + [None] * (n - len(b))
    return _bitonic_merge_desc([_vmax(a[i], b[n - 1 - i]) for i in range(n)])


def _topk_kernel(q_ref, keys_ref, th_ref, e1_ref, s2_ref, e2_ref, s_scr):
    heads = th_ref.shape[0]
    nk = keys_ref.shape[1]
    k = PEER_TOPK
    t = q_ref.shape[0]
    sub = lax.broadcasted_iota(jnp.int32, (SUBLANES, t), 0)
    tops = [[None] * k, [None] * k]
    for h in range(heads):
        for p in range(2):
            hp = 2 * h + p
            dk = keys_ref.shape[2]
            s = lax.dot_general(keys_ref[hp], q_ref[:, hp * dk:(hp + 1) * dk],
                                (((1,), (1,)), ((), ())), preferred_element_type=F32)
            s_scr[hp] = s
            groups = _bitonic_sort_desc([s[SUBLANES * a:SUBLANES * (a + 1), :] for a in range(nk // SUBLANES)])
            assert len(groups) == k
            for shift in (4, 2, 1):
                groups = _merge_top(groups, [pltpu.roll(g, shift, 0) for g in groups])
            for a in range(k):
                prev = tops[p][a]
                tops[p][a] = groups[a] if prev is None else jnp.where(sub == h, groups[a], prev)
    v1, v2 = tops
    best = [v1[0] + v2[b] for b in range(k)]
    for a in range(1, k // 2):
        best = _merge_top(best, [v1[a] + v2[b] for b in range(k // (a + 1))])
    best = _merge_top(best, [v1[a] + v2[0] for a in range(k // 2, k)])
    cmax = best[0]
    zsum = jnp.ones_like(cmax)
    for a in range(1, k):
        zsum = zsum + jnp.exp(best[a] - cmax)
    tau = best[k - 1]
    inv_z = 1.0 / zsum
    for h in range(heads):
        s1 = s_scr[2 * h]
        s2 = s_scr[2 * h + 1]
        tau_h = tau[h:h + 1, :]
        slack = (jnp.abs(tau_h) + jnp.abs(s1)) * (2.0 ** -22)
        th_ref[h] = (tau_h - s1) - slack
        e1_ref[h] = jnp.exp(s1 - v1[0][h:h + 1, :])
        s2_ref[h] = s2
        e2_ref[h] = jnp.exp(s2 - v2[0][h:h + 1, :]) * inv_z[h:h + 1, :]


def _topk_call(qk, keys):
    m = qk.shape[0]
    hp, nk, dk = keys.shape
    heads = hp // 2
    t = _tile(m, 512)
    out = jax.ShapeDtypeStruct((heads, nk, m), F32)
    ospec = pl.BlockSpec((heads, nk, t), lambda i: (0, 0, i))
    return pl.pallas_call(
        _topk_kernel,
        grid=(m // t,),
        in_specs=[pl.BlockSpec((t, hp * dk), lambda i: (i, 0)),
                  pl.BlockSpec((hp, nk, dk), lambda i: (0, 0, 0))],
        out_specs=[ospec, ospec, ospec, ospec],
        out_shape=[out, out, out, out],
        scratch_shapes=[pltpu.VMEM((hp, nk, t), F32)],
        compiler_params=_params(("parallel",)),
        name="peer_topk",
    )(qk, keys)


def _peer_weights(i, a, g_out, row0, bc_scr, th_ref, e1_ref, s2_ref, e2_ref):
    heads, nk, t = th_ref.shape
    for h in range(heads):
        bc_scr[0, h] = jnp.broadcast_to(th_ref[h, pl.ds(i, 1), :], (SUBLANES, t))
        bc_scr[1, h] = jnp.broadcast_to(e1_ref[h, pl.ds(i, 1), :], (SUBLANES, t))
    for tl in range(t // LANES):
        lanes = slice(tl * LANES, (tl + 1) * LANES)
        w = None
        for h in range(heads):
            th = jnp.concatenate([bc_scr[0, h, :, lanes]] * (nk // SUBLANES), axis=0)
            e1 = jnp.concatenate([bc_scr[1, h, :, lanes]] * (nk // SUBLANES), axis=0)
            term = jnp.where(s2_ref[h, :, lanes] >= th, e2_ref[h, :, lanes], 0.0) * e1
            w = term if w is None else w + term
        a_t = a[:, lanes]
        act = 0.5 * a_t * (1.0 + lax.erf(a_t * (1.0 / math.sqrt(2.0))))
        g_out[row0:row0 + nk, lanes] = (w * act).astype(BF16)


def _peer_kernel(x_ref, g_ref, sh_ref, sc_ref, gt_ref, gfin_ref, th_ref, e1_ref, s2_ref, e2_ref,
                 u_ref, vtp_ref, vtc_ref, vtl_ref, o_ref, ht_scr, ga_scr, gb_scr, acc_scr, bc_scr,
                 *, final_norm):
    kb = pl.program_id(1)
    nk = th_ref.shape[1]
    n_sub = PEER_EB // PEER_SUB
    sel = (th_ref, e1_ref, s2_ref, e2_ref)

    @pl.when(kb == 0)
    def _():
        h = _rmsnorm_mod(x_ref[...], g_ref[...], sh_ref[...], sc_ref[...])
        ht_scr[...] = h.T.astype(BF16)
        gb_scr[...] = jnp.zeros(gb_scr.shape, BF16)
        acc_scr[...] = jnp.zeros(acc_scr.shape, F32)

    def first_matmul(idx):
        rows = slice(idx * PEER_SUB, (idx + 1) * PEER_SUB)
        return jnp.dot(u_ref[rows, :], ht_scr[...], preferred_element_type=F32)

    a_next = first_matmul(0)
    for idx in range(2 * n_sub):
        phase, sb = divmod(idx, n_sub)
        a = a_next
        if idx + 1 < 2 * n_sub:
            a_next = first_matmul(idx + 1)
        if sb == min(1, n_sub - 1):
            vt_ref, g_in = (vtp_ref, gb_scr) if phase == 0 else (vtc_ref, ga_scr)
            acc_scr[...] += jnp.dot(vt_ref[...], g_in[...], preferred_element_type=F32)
        g_out = ga_scr if phase == 0 else gb_scr
        for gi in range(PEER_SUB // nk):
            i = kb * (2 * PEER_EB // nk) + idx * (PEER_SUB // nk) + gi
            _peer_weights(i, a[gi * nk:(gi + 1) * nk, :], g_out, sb * PEER_SUB + gi * nk, bc_scr, *sel)

    @pl.when(kb == pl.num_programs(1) - 1)
    def _():
        acc = acc_scr[...] + jnp.dot(vtl_ref[...], gb_scr[...], preferred_element_type=F32)
        y = x_ref[...] + gt_ref[...] * acc.T
        if final_norm:
            y = y * lax.rsqrt(jnp.mean(y * y, axis=-1, keepdims=True) + EPS) * gfin_ref[...]
        o_ref[...] = y


def _peer_call(x2, g, sh, sc, gt, g_final, th, e1, s2, e2, u, vt, rows_per_mod, final_norm):
    m, d = x2.shape
    heads, nk, _ = th.shape
    e = u.shape[0]
    t = _tile(rows_per_mod, 512)
    per = rows_per_mod // t
    eb = PEER_EB
    nkb = e // (2 * eb)
    assert nkb * 2 * eb == e
    sel = pl.BlockSpec((heads, nk, t), lambda i, k: (0, 0, i))
    mod = pl.BlockSpec((None, 1, d), lambda i, k: (i // per, 0, 0))
    vec = pl.BlockSpec((1, d), lambda i, k: (0, 0))
    return pl.pallas_call(
        functools.partial(_peer_kernel, final_norm=final_norm),
        grid=(m // t, nkb),
        in_specs=[pl.BlockSpec((t, d), lambda i, k: (i, 0)),
                  vec, mod, mod, mod, vec, sel, sel, sel, sel,
                  pl.BlockSpec((2 * eb, d), lambda i, k: (k, 0)),
                  pl.BlockSpec((d, eb), lambda i, k: (0, jnp.maximum(2 * k - 1, 0))),
                  pl.BlockSpec((d, eb), lambda i, k: (0, 2 * k)),
                  pl.BlockSpec((d, eb), lambda i, k: (0, 2 * nkb - 1))],
        out_specs=pl.BlockSpec((t, d), lambda i, k: (i, 0)),
        out_shape=jax.ShapeDtypeStruct((m, d), F32),
        scratch_shapes=[pltpu.VMEM((d, t), BF16), pltpu.VMEM((eb, t), BF16),
                        pltpu.VMEM((eb, t), BF16), pltpu.VMEM((d, t), F32),
                        pltpu.VMEM((2, heads, SUBLANES, t), F32)],
        compiler_params=_params(("parallel", "arbitrary")),
        name="peer_experts",
    )(x2, g, sh, sc, gt, g_final, th, e1, s2, e2, u, vt, vt, vt)


def _peer_layer(x2, g_ffn, sh, sc, gt, g_final, w_q, keys, u, v, seq, final_norm):
    qk = _nmm_call(x2, g_ffn, sh, sc, w_q.astype(BF16), seq, BF16)
    kb = keys.reshape(2 * PEER_HEADS, PEER_NKEYS, PEER_DKEY // 2).astype(BF16)
    th, e1, s2, e2 = _topk_call(qk, kb)
    return _peer_call(x2, g_ffn, sh, sc, gt, g_final, th, e1, s2, e2,
                      u.astype(BF16), v.astype(BF16).T, seq, final_norm)


def kernel(x, c, ctx, c_ctx, w_ada, b_ada, g_mix, g_ffn, hg_w_in, hg_w_out, hg_gnorm, hg_lb,
           cv_w_pw1, cv_b_pw1, cv_w_dw, cv_b_dw, cv_ln_g, cv_ln_b, cv_w_pw2, cv_b_pw2,
           peer_w_q, peer_keys, peer_u, peer_v, g_final):
    b, seq, d = x.shape
    ctx_len = ctx.shape[1]
    depth = w_ada.shape[0]
    heads = d // LANES
    assert heads * LANES == d and seq % SCAN_CHUNK == 0 and ctx_len % SCAN_CHUNK == 0

    pad_rows = (-(b + 1)) % SUBLANES
    cc = jnp.concatenate([c, c_ctx[None, :], jnp.zeros((pad_rows, d), F32)], axis=0)
    ada = _ada_call(cc, w_ada, b_ada)

    x2 = x.reshape(b * seq, d)
    for l in range(depth):
        mixer = l % N_MIXERS
        mods = ada[l, :b].reshape(b, 1, 6, d)
        sh_m, sc_m, gt_m, sh_f, sc_f, gt_f = [mods[:, :, i, :] for i in range(6)]
        g_mix_l = g_mix[l][None, :]
        g_ffn_l = g_ffn[l][None, :]
        if mixer == 0:
            a = l // N_MIXERS
            w_in = hg_w_in[a].astype(BF16)
            cmods = ada[l, b].reshape(6, d)
            csh = jnp.broadcast_to(cmods[0][None, None, :], (b, 1, d))
            csc = jnp.broadcast_to(cmods[1][None, None, :], (b, 1, d))
            zc = _nmm_call(ctx.reshape(b * ctx_len, d), g_mix_l, csh, csc, w_in, ctx_len, F32)
            s0 = jnp.zeros((b, heads // 2, LANES, 2 * LANES), F32)
            _, _, s_f, s_b = _scan_call(zc.reshape(b, ctx_len, 5 * d), hg_lb, s0, s0, l)
            z = _nmm_call(x2, g_mix_l, sh_m, sc_m, w_in, seq, F32)
            o_f, o_b, _, _ = _scan_call(z.reshape(b, seq, 5 * d), hg_lb, s_f, s_b, l)
            x2 = _mix_out_call(o_f.reshape(b * seq, d), o_b.reshape(b * seq, d), z,
                               hg_gnorm[a][None, :], hg_w_out[a].astype(BF16), x2, gt_m, seq)
        else:
            bi = l // N_MIXERS
            y = _nmm_call(x2, g_mix_l, sh_m, sc_m, cv_w_pw1[bi].astype(BF16), seq, F32,
                          glu_bias=cv_b_pw1[bi][None, :])
            y = _conv_call(y.reshape(b, seq, d), cv_w_dw[bi])
            x2 = _ln_out_call(y.reshape(b * seq, d), cv_b_dw[bi][None, :], cv_ln_g[bi][None, :],
                              cv_ln_b[bi][None, :], cv_w_pw2[bi].astype(BF16),
                              cv_b_pw2[bi][None, :], x2, gt_m, seq)
        x2 = _peer_layer(x2, g_ffn_l, sh_f, sc_f, gt_f, g_final[None, :], peer_w_q[l], peer_keys[l],
                         peer_u[l], peer_v[l], seq, final_norm=(l == depth - 1))
    return x2.reshape(b, seq, d)
```

```python
import functools
import math

import numpy as np
import jax
import jax.numpy as jnp
from jax import lax
from jax.experimental import pallas as pl
from jax.experimental.pallas import tpu as pltpu

EPS = 1e-6
GRID_W = 64
CONV_WIDTH = 31
HG_HEADS = 8
PEER_HEADS = 8
PEER_NKEYS = 128
PEER_DKEY = 256
PEER_TOPK = 16
N_MIXERS = 2

LANES = 128
SUBLANES = 8
SCAN_CHUNK = 128
SCAN_MATMUL_LEVELS = int(math.log2(SUBLANES))
PEER_EB = 1024
PEER_SUB = 128
MXU_WIDTH = 256
VMEM_LIMIT = 56 * 1024 * 1024

F32 = jnp.float32
BF16 = jnp.bfloat16


def _tile(n, pref):
    t = min(n, pref)
    while n % t:
        t //= 2
    return t


def _params(sem):
    return pltpu.CompilerParams(dimension_semantics=sem, vmem_limit_bytes=VMEM_LIMIT)


def _rmsnorm_mod(x, g, sh, sc):
    y = x * lax.rsqrt(jnp.mean(x * x, axis=-1, keepdims=True) + EPS) * g
    return y * (1.0 + sc) + sh


def _ada_kernel(c_ref, w_ref, b_ref, o_ref):
    s = c_ref[...]
    s = s * jax.nn.sigmoid(s)
    o_ref[...] = jnp.dot(s, w_ref[...], preferred_element_type=F32,
                         precision=lax.Precision.HIGHEST) + b_ref[...]


def _ada_call(cc, w_ada, b_ada):
    depth, d, n = w_ada.shape
    r = cc.shape[0]
    tn = _tile(n, 1024)
    return pl.pallas_call(
        _ada_kernel,
        grid=(depth, n // tn),
        in_specs=[pl.BlockSpec((r, d), lambda l, j: (0, 0)),
                  pl.BlockSpec((None, d, tn), lambda l, j: (l, 0, j)),
                  pl.BlockSpec((None, 1, tn), lambda l, j: (l, 0, j))],
        out_specs=pl.BlockSpec((None, r, tn), lambda l, j: (l, 0, j)),
        out_shape=jax.ShapeDtypeStruct((depth, r, n), F32),
        compiler_params=_params(("parallel", "parallel")),
        name="ada",
    )(cc, w_ada, b_ada.reshape(depth, 1, n))


def _nmm_kernel(x_ref, g_ref, sh_ref, sc_ref, w_ref, o_ref, h_scr):
    @pl.when(pl.program_id(1) == 0)
    def _():
        h_scr[...] = _rmsnorm_mod(x_ref[...], g_ref[...], sh_ref[...], sc_ref[...]).astype(BF16)

    o_ref[...] = jnp.dot(h_scr[...], w_ref[...], preferred_element_type=F32).astype(o_ref.dtype)


def _nmm_glu_kernel(x_ref, g_ref, sh_ref, sc_ref, wa_ref, wg_ref, ba_ref, bg_ref, o_ref, h_scr):
    @pl.when(pl.program_id(1) == 0)
    def _():
        h_scr[...] = _rmsnorm_mod(x_ref[...], g_ref[...], sh_ref[...], sc_ref[...]).astype(BF16)

    h = h_scr[...]
    a = jnp.dot(h, wa_ref[...], preferred_element_type=F32) + ba_ref[...]
    g = jnp.dot(h, wg_ref[...], preferred_element_type=F32) + bg_ref[...]
    o_ref[...] = (a * jax.nn.sigmoid(g)).astype(o_ref.dtype)


def _nmm_call(x2, g, sh, sc, w, rows_per_mod, out_dtype, glu_bias=None):
    m, d = x2.shape
    n = w.shape[1]
    tm = _tile(rows_per_mod, 1024)
    per = rows_per_mod // tm
    x_spec = pl.BlockSpec((tm, d), lambda i, j: (i, 0))
    g_spec = pl.BlockSpec((1, d), lambda i, j: (0, 0))
    mod_spec = pl.BlockSpec((None, 1, d), lambda i, j: (i // per, 0, 0))
    scratch = [pltpu.VMEM((tm, d), BF16)]
    if glu_bias is None:
        tn = _tile(n, 1024)
        return pl.pallas_call(
            _nmm_kernel,
            grid=(m // tm, n // tn),
            in_specs=[x_spec, g_spec, mod_spec, mod_spec,
                      pl.BlockSpec((d, tn), lambda i, j: (0, j))],
            out_specs=pl.BlockSpec((tm, tn), lambda i, j: (i, j)),
            out_shape=jax.ShapeDtypeStruct((m, n), out_dtype),
            scratch_shapes=scratch,
            compiler_params=_params(("parallel", "arbitrary")),
            name="norm_mod_matmul",
        )(x2, g, sh, sc, w)
    nh = n // 2
    tn = _tile(nh, 512)
    nb = nh // tn
    return pl.pallas_call(
        _nmm_glu_kernel,
        grid=(m // tm, nb),
        in_specs=[x_spec, g_spec, mod_spec, mod_spec,
                  pl.BlockSpec((d, tn), lambda i, j: (0, j)),
                  pl.BlockSpec((d, tn), lambda i, j: (0, j + nb)),
                  pl.BlockSpec((1, tn), lambda i, j: (0, j)),
                  pl.BlockSpec((1, tn), lambda i, j: (0, j + nb))],
        out_specs=pl.BlockSpec((tm, tn), lambda i, j: (i, j)),
        out_shape=jax.ShapeDtypeStruct((m, nh), out_dtype),
        scratch_shapes=scratch,
        compiler_params=_params(("parallel", "arbitrary")),
        name="norm_mod_matmul_glu",
    )(x2, g, sh, sc, w, w, glu_bias, glu_bias)


@functools.lru_cache(maxsize=None)
def _scan_consts(c):
    nl = int(math.log2(c))
    assert 1 << nl == c
    idx = np.arange(c)
    mats = np.zeros((2, 2 + SCAN_MATMUL_LEVELS, c, c), np.float32)
    level = np.zeros((2, c, c), np.int32)
    for d in range(2):
        tau = idx if d == 0 else c - 1 - idx
        tt, tr = tau[:, None], tau[None, :]
        mats[d, 0] = tr <= tt
        mats[d, 1] = tr > tt
        for l in range(SCAN_MATMUL_LEVELS):
            half = 1 << l
            mid = (tau // (2 * half)) * (2 * half) + half - 1
            later = ((tau >> l) & 1) == 1
            mats[d, 2 + l] = ((later[:, None] & (tr > mid[:, None]) & (tr <= tt))
                              | ((~later)[:, None] & (tr > tt) & (tr <= mid[:, None])))
        xor = tt ^ tr
        lev = np.floor(np.log2(np.maximum(xor, 1))).astype(np.int32)
        level[d] = np.where(tt > tr, lev, np.where(tt == tr, -1, -2))
    return mats.reshape(2, (2 + SCAN_MATMUL_LEVELS) * c, c), level, nl


def _scan_direction(d, zf, v, q, lb, est_ref, lev_ref, st_ref, o_ref, n_levels):
    c = zf.shape[0]
    f = lb + (1.0 - lb) * jax.nn.sigmoid(zf)
    logf = jnp.log(f)
    kk = (1.0 - lb) * jax.nn.sigmoid(-zf)
    lev = lev_ref[d]

    def block_diag(x):
        zero = jnp.zeros((x.shape[0], LANES), BF16)
        return jnp.concatenate([jnp.concatenate([x[:, :LANES], zero], axis=1),
                                jnp.concatenate([zero, x[:, LANES:]], axis=1)], axis=0)

    nt = (((1,), (1,)), ((), ()))
    for hp in range(st_ref.shape[0]):
        sl = slice(2 * hp * LANES, 2 * (hp + 1) * LANES)
        lf = logf[:, sl]
        hi = lf.astype(BF16)
        mid = (lf - hi.astype(F32)).astype(BF16)
        ex = jnp.dot(est_ref[d], jnp.concatenate([hi, mid], axis=0), preferred_element_type=F32)
        p = jnp.exp(ex)
        dec = jnp.exp(ex[0:1, :] + ex[c:c + 1, :])
        q_h, k_h, v_h = q[:, sl], kk[:, sl], v[:, sl]
        st = st_ref[hp]
        qb = (q_h * p[0:c]).astype(BF16)
        inter = lax.dot_general(qb, block_diag(st.astype(BF16)), nt, preferred_element_type=F32)
        pr = lax.dot_general(q_h.astype(BF16), block_diag(k_h.astype(BF16)), nt,
                             preferred_element_type=F32)
        scores = jnp.where(lev == -1, pr, 0.0)
        b = ex[0:c]
        for l in range(n_levels):
            if l < SCAN_MATMUL_LEVELS:
                pl_ = p[(2 + l) * c:(3 + l) * c]
            else:
                half = 1 << l
                rows = [jnp.broadcast_to(b[p0 + half - 1 + d:p0 + half + d, :], (2 * half, 2 * LANES))
                        for p0 in range(0, c, 2 * half)]
                bmid = jnp.concatenate(rows, axis=0) if len(rows) > 1 else rows[0]
                pl_ = jnp.exp(-jnp.abs(b - bmid))
            ql = (q_h * pl_).astype(BF16)
            kl = (k_h * pl_).astype(BF16)
            pr = lax.dot_general(ql, block_diag(kl), nt, preferred_element_type=F32)
            scores = jnp.where(lev == l, pr, scores)
        intra = jnp.dot(scores.astype(BF16), block_diag(v_h.astype(BF16)), preferred_element_type=F32)
        o_ref[:, sl] = inter + intra
        kend = (k_h * p[c:2 * c]).astype(BF16)
        vt = jnp.concatenate([v_h[:, :LANES].T, v_h[:, LANES:].T], axis=1).astype(BF16)
        st_ref[hp] = st * dec + jnp.dot(vt, block_diag(kend), preferred_element_type=F32)


def _scan_kernel(lbraw_ref, est_ref, lev_ref, s0f_ref, s0b_ref,
                 vf_ref, ff_ref, qf_ref, vb_ref, fb_ref, qb_ref,
                 of_ref, ob_ref, sf_ref, sb_ref, stf_scr, stb_scr, *, layer, n_levels):
    j = pl.program_id(1)

    @pl.when(j == 0)
    def _():
        stf_scr[...] = s0f_ref[...]
        stb_scr[...] = s0b_ref[...]

    raw = lbraw_ref[...]
    e = jnp.exp(raw - jnp.max(raw, axis=0, keepdims=True))
    lb_all = jnp.sum(e[0:layer + 1], axis=0, keepdims=True) / jnp.sum(e, axis=0, keepdims=True)
    d_model = vf_ref.shape[-1]
    _scan_direction(0, ff_ref[...], vf_ref[...], qf_ref[...], lb_all[:, :d_model],
                    est_ref, lev_ref, stf_scr, of_ref, n_levels)
    _scan_direction(1, fb_ref[...], vb_ref[...], qb_ref[...], lb_all[:, d_model:],
                    est_ref, lev_ref, stb_scr, ob_ref, n_levels)

    @pl.when(j == pl.num_programs(1) - 1)
    def _():
        sf_ref[...] = stf_scr[...]
        sb_ref[...] = stb_scr[...]


def _scan_call(z, hg_lb, s0f, s0b, layer):
    b, l, d5 = z.shape
    d = d5 // 5
    pairs = d // (2 * LANES)
    c = SCAN_CHUNK
    nc = l // c
    est, lev, n_levels = _scan_consts(c)
    est = jnp.asarray(np.concatenate([est, est], axis=2), BF16)
    lev = jnp.asarray(np.concatenate([lev, lev], axis=2))

    def zspec(col, rev):
        if rev:
            return pl.BlockSpec((None, c, d), lambda bi, j: (bi, nc - 1 - j, col))
        return pl.BlockSpec((None, c, d), lambda bi, j: (bi, j, col))

    full = lambda a: pl.BlockSpec(a.shape, lambda bi, j: (0,) * a.ndim)
    st_spec = pl.BlockSpec((None, pairs, LANES, 2 * LANES), lambda bi, j: (bi, 0, 0, 0))
    return pl.pallas_call(
        functools.partial(_scan_kernel, layer=layer, n_levels=n_levels),
        grid=(b, nc),
        in_specs=[full(hg_lb), full(est), full(lev), st_spec, st_spec,
                  zspec(0, False), zspec(1, False), zspec(3, False),
                  zspec(0, True), zspec(2, True), zspec(3, True)],
        out_specs=[pl.BlockSpec((None, c, d), lambda bi, j: (bi, j, 0)),
                   pl.BlockSpec((None, c, d), lambda bi, j: (bi, nc - 1 - j, 0)),
                   st_spec, st_spec],
        out_shape=[jax.ShapeDtypeStruct((b, l, d), F32), jax.ShapeDtypeStruct((b, l, d), F32),
                   jax.ShapeDtypeStruct(s0f.shape, F32), jax.ShapeDtypeStruct(s0b.shape, F32)],
        scratch_shapes=[pltpu.VMEM((pairs, LANES, 2 * LANES), F32),
                        pltpu.VMEM((pairs, LANES, 2 * LANES), F32)],
        compiler_params=_params(("parallel", "arbitrary")),
        name="hgrn2_scan",
    )(hg_lb, est, lev, s0f, s0b, z, z, z, z, z, z)


def _mix_out_kernel(of_ref, ob_ref, gate_ref, gn_ref, w_ref, x_ref, gt_ref, o_ref, h_scr):
    @pl.when(pl.program_id(1) == 0)
    def _():
        o = of_ref[...] + ob_ref[...]
        gate = gate_ref[...]
        gn = gn_ref[...]
        for h in range(o.shape[1] // LANES):
            sl = slice(h * LANES, (h + 1) * LANES)
            oh = o[:, sl]
            y = oh * lax.rsqrt(jnp.mean(oh * oh, axis=-1, keepdims=True) + EPS) * gn
            gh = gate[:, sl]
            h_scr[:, sl] = (y * (gh * jax.nn.sigmoid(gh))).astype(BF16)

    mix = jnp.dot(h_scr[...], w_ref[...], preferred_element_type=F32)
    o_ref[...] = x_ref[...] + gt_ref[...] * mix


def _mix_out_call(o_f, o_b, z2, gnorm, w_out, x2, gt, rows_per_mod):
    m, d = x2.shape
    tm = _tile(rows_per_mod, 1024)
    per = rows_per_mod // tm
    tn = _tile(d, 512)
    row = pl.BlockSpec((tm, d), lambda i, j: (i, 0))
    return pl.pallas_call(
        _mix_out_kernel,
        grid=(m // tm, d // tn),
        in_specs=[row, row,
                  pl.BlockSpec((tm, d), lambda i, j: (i, 4)),
                  pl.BlockSpec((1, LANES), lambda i, j: (0, 0)),
                  pl.BlockSpec((d, tn), lambda i, j: (0, j)),
                  pl.BlockSpec((tm, tn), lambda i, j: (i, j)),
                  pl.BlockSpec((None, 1, tn), lambda i, j: (i // per, 0, j))],
        out_specs=pl.BlockSpec((tm, tn), lambda i, j: (i, j)),
        out_shape=jax.ShapeDtypeStruct((m, d), F32),
        scratch_shapes=[pltpu.VMEM((tm, d), BF16)],
        compiler_params=_params(("parallel", "arbitrary")),
        name="hgrn2_out",
    )(o_f, o_b, z2, gnorm, w_out, x2, gt)


def _conv_kernel(y_ref, w_ref, o_ref, padw_scr, padh_scr, *, rows, n_w_tiles):
    ct = pl.program_id(1)
    half = (CONV_WIDTH - 1) // 2
    l, lanes = y_ref.shape
    lead = padw_scr.shape[1] - GRID_W - 16

    @pl.when(ct < n_w_tiles)
    def _():
        padw_scr[...] = jnp.zeros(padw_scr.shape, F32)
        padw_scr[:, lead:lead + GRID_W, :] = y_ref[...].reshape(rows, GRID_W, lanes)
        acc = jnp.zeros((rows, GRID_W, lanes), F32)
        for k in range(CONV_WIDTH):
            start = lead + k - half
            acc = acc + w_ref[k:k + 1, :].reshape(1, 1, lanes) * padw_scr[:, start:start + GRID_W, :]
        o_ref[...] = acc.reshape(l, lanes)

    @pl.when(ct >= n_w_tiles)
    def _():
        margin = half * GRID_W
        padh_scr[...] = jnp.zeros(padh_scr.shape, F32)
        padh_scr[margin:margin + l, :] = y_ref[...]
        acc = jnp.zeros((l, lanes), F32)
        for k in range(CONV_WIDTH):
            acc = acc + w_ref[k:k + 1, :] * padh_scr[k * GRID_W:k * GRID_W + l, :]
        o_ref[...] = acc


def _conv_call(y, w_dw):
    b, l, c = y.shape
    rows = l // GRID_W
    half = (CONV_WIDTH - 1) // 2
    n_w_tiles = (c // 2) // LANES
    wp = jnp.pad(w_dw, ((0, 32 - CONV_WIDTH), (0, 0)))
    return pl.pallas_call(
        functools.partial(_conv_kernel, rows=rows, n_w_tiles=n_w_tiles),
        grid=(b, c // LANES),
        in_specs=[pl.BlockSpec((None, l, LANES), lambda bi, ct: (bi, 0, ct)),
                  pl.BlockSpec((32, LANES), lambda bi, ct: (0, ct))],
        out_specs=pl.BlockSpec((None, l, LANES), lambda bi, ct: (bi, 0, ct)),
        out_shape=jax.ShapeDtypeStruct((b, l, c), F32),
        scratch_shapes=[pltpu.VMEM((rows, GRID_W + 32, LANES), F32),
                        pltpu.VMEM((l + 2 * half * GRID_W, LANES), F32)],
        compiler_params=_params(("parallel", "parallel")),
        name="axial_dwconv",
    )(y, wp)


def _ln_out_kernel(y_ref, bdw_ref, lg_ref, lb_ref, w_ref, b2_ref, x_ref, gt_ref, o_ref, h_scr):
    @pl.when(pl.program_id(1) == 0)
    def _():
        y = y_ref[...] + bdw_ref[...]
        yc = y - jnp.mean(y, axis=-1, keepdims=True)
        yn = yc * lax.rsqrt(jnp.mean(yc * yc, axis=-1, keepdims=True) + EPS)
        yn = yn * lg_ref[...] + lb_ref[...]
        h_scr[...] = (yn * jax.nn.sigmoid(yn)).astype(BF16)

    mix = jnp.dot(h_scr[...], w_ref[...], preferred_element_type=F32) + b2_ref[...]
    o_ref[...] = x_ref[...] + gt_ref[...] * mix


def _ln_out_call(y2, b_dw, ln_g, ln_b, w_pw2, b_pw2, x2, gt, rows_per_mod):
    m, d = x2.shape
    tm = _tile(rows_per_mod, 1024)
    per = rows_per_mod // tm
    tn = _tile(d, 512)
    vec = pl.BlockSpec((1, d), lambda i, j: (0, 0))
    return pl.pallas_call(
        _ln_out_kernel,
        grid=(m // tm, d // tn),
        in_specs=[pl.BlockSpec((tm, d), lambda i, j: (i, 0)), vec, vec, vec,
                  pl.BlockSpec((d, tn), lambda i, j: (0, j)),
                  pl.BlockSpec((1, tn), lambda i, j: (0, j)),
                  pl.BlockSpec((tm, tn), lambda i, j: (i, j)),
                  pl.BlockSpec((None, 1, tn), lambda i, j: (i // per, 0, j))],
        out_specs=pl.BlockSpec((tm, tn), lambda i, j: (i, j)),
        out_shape=jax.ShapeDtypeStruct((m, d), F32),
        scratch_shapes=[pltpu.VMEM((tm, d), BF16)],
        compiler_params=_params(("parallel", "arbitrary")),
        name="conv_out",
    )(y2, b_dw, ln_g, ln_b, w_pw2, b_pw2, x2, gt)


def _vmax(a, b):
    if a is None:
        return b
    if b is None:
        return a
    return jnp.maximum(a, b)


def _vmin(a, b):
    if a is None or b is None:
        return None
    return jnp.minimum(a, b)


def _bitonic_merge_desc(xs):
    n = len(xs)
    j = n // 2
    while j >= 1:
        for i in range(n):
            o = i ^ j
            if o > i:
                a, b = xs[i], xs[o]
                xs[i], xs[o] = _vmax(a, b), _vmin(a, b)
        j //= 2
    return xs


def _bitonic_sort_desc(xs):
    n = len(xs)
    k = 2
    while k <= n:
        j = k // 2
        while j >= 1:
            for i in range(n):
                o = i ^ j
                if o > i:
                    a, b = xs[i], xs[o]
                    if (i & k) == 0:
                        xs[i], xs[o] = _vmax(a, b), _vmin(a, b)
                    else:
                        xs[i], xs[o] = _vmin(a, b), _vmax(a, b)
            j //= 2
        k *= 2
    return xs


def _merge_top(a, b):
    n = len(a)
    b = list(b) + [None] * (n - len(b))
    return _bitonic_merge_desc([_vmax(a[i], b[n - 1 - i]) for i in range(n)])


def _topk_kernel(q_ref, keys_ref, th_ref, e1_ref, s2_ref, e2_ref, s_scr):
    heads = th_ref.shape[0]
    nk = keys_ref.shape[1]
    k = PEER_TOPK
    t = q_ref.shape[0]
    sub = lax.broadcasted_iota(jnp.int32, (SUBLANES, t), 0)
    tops = [[None] * k, [None] * k]
    for h in range(heads):
        for p in range(2):
            hp = 2 * h + p
            dk = keys_ref.shape[2]
            s = lax.dot_general(keys_ref[hp], q_ref[:, hp * dk:(hp + 1) * dk],
                                (((1,), (1,)), ((), ())), preferred_element_type=F32)
            s_scr[hp] = s
            groups = _bitonic_sort_desc([s[SUBLANES * a:SUBLANES * (a + 1), :] for a in range(nk // SUBLANES)])
            assert len(groups) == k
            for shift in (4, 2, 1):
                groups = _merge_top(groups, [pltpu.roll(g, shift, 0) for g in groups])
            for a in range(k):
                prev = tops[p][a]
                tops[p][a] = groups[a] if prev is None else jnp.where(sub == h, groups[a], prev)
    v1, v2 = tops
    best = [v1[0] + v2[b] for b in range(k)]
    for a in range(1, k // 2):
        best = _merge_top(best, [v1[a] + v2[b] for b in range(k // (a + 1))])
    best = _merge_top(best, [v1[a] + v2[0] for a in range(k // 2, k)])
    cmax = best[0]
    zsum = jnp.ones_like(cmax)
    for a in range(1, k):
        zsum = zsum + jnp.exp(best[a] - cmax)
    tau = best[k - 1]
    inv_z = 1.0 / zsum
    for h in range(heads):
        s1 = s_scr[2 * h]
        s2 = s_scr[2 * h + 1]
        tau_h = tau[h:h + 1, :]
        slack = (jnp.abs(tau_h) + jnp.abs(s1)) * (2.0 ** -22)
        th_ref[h] = (tau_h - s1) - slack
        e1_ref[h] = jnp.exp(s1 - v1[0][h:h + 1, :])
        s2_ref[h] = s2
        e2_ref[h] = jnp.exp(s2 - v2[0][h:h + 1, :]) * inv_z[h:h + 1, :]


def _topk_call(qk, keys):
    m = qk.shape[0]
    hp, nk, dk = keys.shape
    heads = hp // 2
    t = _tile(m, 512)
    out = jax.ShapeDtypeStruct((heads, nk, m), F32)
    ospec = pl.BlockSpec((heads, nk, t), lambda i: (0, 0, i))
    return pl.pallas_call(
        _topk_kernel,
        grid=(m // t,),
        in_specs=[pl.BlockSpec((t, hp * dk), lambda i: (i, 0)),
                  pl.BlockSpec((hp, nk, dk), lambda i: (0, 0, 0))],
        out_specs=[ospec, ospec, ospec, ospec],
        out_shape=[out, out, out, out],
        scratch_shapes=[pltpu.VMEM((hp, nk, t), F32)],
        compiler_params=_params(("parallel",)),
        name="peer_topk",
    )(qk, keys)


def _peer_weights(i, a, g_out, row0, bc_scr, th_ref, e1_ref, s2_ref, e2_ref):
    heads, nk, t = th_ref.shape
    for h in range(heads):
        bc_scr[0, h] = jnp.broadcast_to(th_ref[h, pl.ds(i, 1), :], (SUBLANES, t))
        bc_scr[1, h] = jnp.broadcast_to(e1_ref[h, pl.ds(i, 1), :], (SUBLANES, t))
    for tl in range(t // LANES):
        lanes = slice(tl * LANES, (tl + 1) * LANES)
        w = None
        for h in range(heads):
            th = jnp.concatenate([bc_scr[0, h, :, lanes]] * (nk // SUBLANES), axis=0)
            e1 = jnp.concatenate([bc_scr[1, h, :, lanes]] * (nk // SUBLANES), axis=0)
            term = jnp.where(s2_ref[h, :, lanes] >= th, e2_ref[h, :, lanes], 0.0) * e1
            w = term if w is None else w + term
        a_t = a[:, lanes]
        act = 0.5 * a_t * (1.0 + lax.erf(a_t * (1.0 / math.sqrt(2.0))))
        g_out[row0:row0 + nk, lanes] = (w * act).astype(BF16)


def _peer_kernel(x_ref, g_ref, sh_ref, sc_ref, gt_ref, th_ref, e1_ref, s2_ref, e2_ref,
                 u_ref, vtp_ref, vtc_ref, vtl_ref, o_ref, ht_scr, ga_scr, gb_scr, acc_scr, bc_scr):
    kb = pl.program_id(1)
    nk = th_ref.shape[1]
    n_sub = PEER_EB // PEER_SUB
    sel = (th_ref, e1_ref, s2_ref, e2_ref)

    @pl.when(kb == 0)
    def _():
        h = _rmsnorm_mod(x_ref[...], g_ref[...], sh_ref[...], sc_ref[...])
        ht_scr[...] = h.T.astype(BF16)
        gb_scr[...] = jnp.zeros(gb_scr.shape, BF16)
        acc_scr[...] = jnp.zeros(acc_scr.shape, F32)

    def first_matmul(idx):
        rows = slice(idx * PEER_SUB, (idx + 1) * PEER_SUB)
        return jnp.dot(u_ref[rows, :], ht_scr[...], preferred_element_type=F32)

    a_next = first_matmul(0)
    for idx in range(2 * n_sub):
        phase, sb = divmod(idx, n_sub)
        a = a_next
        if idx + 1 < 2 * n_sub:
            a_next = first_matmul(idx + 1)
        if sb == min(1, n_sub - 1):
            vt_ref, g_in = (vtp_ref, gb_scr) if phase == 0 else (vtc_ref, ga_scr)
            acc_scr[...] += jnp.dot(vt_ref[...], g_in[...], preferred_element_type=F32)
        g_out = ga_scr if phase == 0 else gb_scr
        for gi in range(PEER_SUB // nk):
            i = kb * (2 * PEER_EB // nk) + idx * (PEER_SUB // nk) + gi
            _peer_weights(i, a[gi * nk:(gi + 1) * nk, :], g_out, sb * PEER_SUB + gi * nk, bc_scr, *sel)

    @pl.when(kb == pl.num_programs(1) - 1)
    def _():
        acc = acc_scr[...] + jnp.dot(vtl_ref[...], gb_scr[...], preferred_element_type=F32)
        o_ref[...] = x_ref[...] + gt_ref[...] * acc.T


def _peer_call(x2, g, sh, sc, gt, th, e1, s2, e2, u, vt, rows_per_mod):
    m, d = x2.shape
    heads, nk, _ = th.shape
    e = u.shape[0]
    t = _tile(rows_per_mod, 512)
    per = rows_per_mod // t
    eb = PEER_EB
    nkb = e // (2 * eb)
    assert nkb * 2 * eb == e
    sel = pl.BlockSpec((heads, nk, t), lambda i, k: (0, 0, i))
    mod = pl.BlockSpec((None, 1, d), lambda i, k: (i // per, 0, 0))
    return pl.pallas_call(
        _peer_kernel,
        grid=(m // t, nkb),
        in_specs=[pl.BlockSpec((t, d), lambda i, k: (i, 0)),
                  pl.BlockSpec((1, d), lambda i, k: (0, 0)),
                  mod, mod, mod, sel, sel, sel, sel,
                  pl.BlockSpec((2 * eb, d), lambda i, k: (k, 0)),
                  pl.BlockSpec((d, eb), lambda i, k: (0, jnp.maximum(2 * k - 1, 0))),
                  pl.BlockSpec((d, eb), lambda i, k: (0, 2 * k)),
                  pl.BlockSpec((d, eb), lambda i, k: (0, 2 * nkb - 1))],
        out_specs=pl.BlockSpec((t, d), lambda i, k: (i, 0)),
        out_shape=jax.ShapeDtypeStruct((m, d), F32),
        scratch_shapes=[pltpu.VMEM((d, t), BF16), pltpu.VMEM((eb, t), BF16),
                        pltpu.VMEM((eb, t), BF16), pltpu.VMEM((d, t), F32),
                        pltpu.VMEM((2, heads, SUBLANES, t), F32)],
        compiler_params=_params(("parallel", "arbitrary")),
        name="peer_experts",
    )(x2, g, sh, sc, gt, th, e1, s2, e2, u, vt, vt, vt)


def _final_kernel(x_ref, g_ref, o_ref):
    x = x_ref[...]
    o_ref[...] = x * lax.rsqrt(jnp.mean(x * x, axis=-1, keepdims=True) + EPS) * g_ref[...]


def _final_call(x2, g):
    m, d = x2.shape
    tm = _tile(m, 1024)
    return pl.pallas_call(
        _final_kernel,
        grid=(m // tm,),
        in_specs=[pl.BlockSpec((tm, d), lambda i: (i, 0)), pl.BlockSpec((1, d), lambda i: (0, 0))],
        out_specs=pl.BlockSpec((tm, d), lambda i: (i, 0)),
        out_shape=jax.ShapeDtypeStruct((m, d), F32),
        compiler_params=_params(("parallel",)),
        name="final_norm",
    )(x2, g)


def _peer_layer(x2, g_ffn, sh, sc, gt, w_q, keys, u, v, seq):
    d = x2.shape[1]
    qk = _nmm_call(x2, g_ffn, sh, sc, w_q.astype(BF16), seq, BF16)
    kb = keys.reshape(2 * PEER_HEADS, PEER_NKEYS, PEER_DKEY // 2).astype(BF16)
    th, e1, s2, e2 = _topk_call(qk, kb)
    return _peer_call(x2, g_ffn, sh, sc, gt, th, e1, s2, e2,
                      u.astype(BF16), v.astype(BF16).T, seq)


def kernel(x, c, ctx, c_ctx, w_ada, b_ada, g_mix, g_ffn, hg_w_in, hg_w_out, hg_gnorm, hg_lb,
           cv_w_pw1, cv_b_pw1, cv_w_dw, cv_b_dw, cv_ln_g, cv_ln_b, cv_w_pw2, cv_b_pw2,
           peer_w_q, peer_keys, peer_u, peer_v, g_final):
    b, seq, d = x.shape
    ctx_len = ctx.shape[1]
    depth = w_ada.shape[0]
    heads = d // LANES
    assert heads * LANES == d and seq % SCAN_CHUNK == 0 and ctx_len % SCAN_CHUNK == 0

    pad_rows = (-(b + 1)) % SUBLANES
    cc = jnp.concatenate([c, c_ctx[None, :], jnp.zeros((pad_rows, d), F32)], axis=0)
    ada = _ada_call(cc, w_ada, b_ada)

    x2 = x.reshape(b * seq, d)
    for l in range(depth):
        mixer = l % N_MIXERS
        mods = ada[l, :b].reshape(b, 1, 6, d)
        sh_m, sc_m, gt_m, sh_f, sc_f, gt_f = [mods[:, :, i, :] for i in range(6)]
        g_mix_l = g_mix[l][None, :]
        g_ffn_l = g_ffn[l][None, :]
        if mixer == 0:
            a = l // N_MIXERS
            w_in = hg_w_in[a].astype(BF16)
            cmods = ada[l, b].reshape(6, d)
            csh = jnp.broadcast_to(cmods[0][None, None, :], (b, 1, d))
            csc = jnp.broadcast_to(cmods[1][None, None, :], (b, 1, d))
            zc = _nmm_call(ctx.reshape(b * ctx_len, d), g_mix_l, csh, csc, w_in, ctx_len, F32)
            s0 = jnp.zeros((b, heads // 2, LANES, 2 * LANES), F32)
            _, _, s_f, s_b = _scan_call(zc.reshape(b, ctx_len, 5 * d), hg_lb, s0, s0, l)
            z = _nmm_call(x2, g_mix_l, sh_m, sc_m, w_in, seq, F32)
            o_f, o_b, _, _ = _scan_call(z.reshape(b, seq, 5 * d), hg_lb, s_f, s_b, l)
            x2 = _mix_out_call(o_f.reshape(b * seq, d), o_b.reshape(b * seq, d), z,
                               hg_gnorm[a][None, :], hg_w_out[a].astype(BF16), x2, gt_m, seq)
        else:
            bi = l // N_MIXERS
            y = _nmm_call(x2, g_mix_l, sh_m, sc_m, cv_w_pw1[bi].astype(BF16), seq, F32,
                          glu_bias=cv_b_pw1[bi][None, :])
            y = _conv_call(y.reshape(b, seq, d), cv_w_dw[bi])
            x2 = _ln_out_call(y.reshape(b * seq, d), cv_b_dw[bi][None, :], cv_ln_g[bi][None, :],
                              cv_ln_b[bi][None, :], cv_w_pw2[bi].astype(BF16),
                              cv_b_pw2[bi][None, :], x2, gt_m, seq)
        x2 = _peer_layer(x2, g_ffn_l, sh_f, sc_f, gt_f, peer_w_q[l], peer_keys[l],
                         peer_u[l], peer_v[l], seq)
    return _final_call(x2, g_final[None, :]).reshape(b, seq, d)
```

```python
import functools
import math

import numpy as np
import jax
import jax.numpy as jnp
from jax import lax
from jax.experimental import pallas as pl
from jax.experimental.pallas import tpu as pltpu

EPS = 1e-6
GRID_W = 64
CONV_WIDTH = 31
HG_HEADS = 8
PEER_HEADS = 8
PEER_NKEYS = 128
PEER_DKEY = 256
PEER_TOPK = 16
N_MIXERS = 2

LANES = 128
SUBLANES = 8
SCAN_CHUNK = 128
SCAN_MATMUL_LEVELS = int(math.log2(SUBLANES))
PEER_EB = 512
PEER_SUB = 256
MXU_WIDTH = 256
VMEM_LIMIT = 56 * 1024 * 1024

F32 = jnp.float32
BF16 = jnp.bfloat16


def _tile(n, pref):
    t = min(n, pref)
    while n % t:
        t //= 2
    return t


def _params(sem):
    return pltpu.CompilerParams(dimension_semantics=sem, vmem_limit_bytes=VMEM_LIMIT)


def _rmsnorm_mod(x, g, sh, sc):
    y = x * lax.rsqrt(jnp.mean(x * x, axis=-1, keepdims=True) + EPS) * g
    return y * (1.0 + sc) + sh


def _ada_kernel(c_ref, w_ref, b_ref, o_ref):
    s = c_ref[...]
    s = s * jax.nn.sigmoid(s)
    o_ref[...] = jnp.dot(s, w_ref[...], preferred_element_type=F32,
                         precision=lax.Precision.HIGHEST) + b_ref[...]


def _ada_call(cc, w_ada, b_ada):
    depth, d, n = w_ada.shape
    r = cc.shape[0]
    tn = _tile(n, 1024)
    return pl.pallas_call(
        _ada_kernel,
        grid=(depth, n // tn),
        in_specs=[pl.BlockSpec((r, d), lambda l, j: (0, 0)),
                  pl.BlockSpec((None, d, tn), lambda l, j: (l, 0, j)),
                  pl.BlockSpec((None, 1, tn), lambda l, j: (l, 0, j))],
        out_specs=pl.BlockSpec((None, r, tn), lambda l, j: (l, 0, j)),
        out_shape=jax.ShapeDtypeStruct((depth, r, n), F32),
        compiler_params=_params(("parallel", "parallel")),
        name="ada",
    )(cc, w_ada, b_ada.reshape(depth, 1, n))


def _nmm_kernel(x_ref, g_ref, sh_ref, sc_ref, w_ref, o_ref, h_scr):
    @pl.when(pl.program_id(1) == 0)
    def _():
        h_scr[...] = _rmsnorm_mod(x_ref[...], g_ref[...], sh_ref[...], sc_ref[...]).astype(BF16)

    o_ref[...] = jnp.dot(h_scr[...], w_ref[...], preferred_element_type=F32).astype(o_ref.dtype)


def _nmm_glu_kernel(x_ref, g_ref, sh_ref, sc_ref, wa_ref, wg_ref, ba_ref, bg_ref, o_ref, h_scr):
    @pl.when(pl.program_id(1) == 0)
    def _():
        h_scr[...] = _rmsnorm_mod(x_ref[...], g_ref[...], sh_ref[...], sc_ref[...]).astype(BF16)

    h = h_scr[...]
    a = jnp.dot(h, wa_ref[...], preferred_element_type=F32) + ba_ref[...]
    g = jnp.dot(h, wg_ref[...], preferred_element_type=F32) + bg_ref[...]
    o_ref[...] = (a * jax.nn.sigmoid(g)).astype(o_ref.dtype)


def _nmm_call(x2, g, sh, sc, w, rows_per_mod, out_dtype, glu_bias=None):
    m, d = x2.shape
    n = w.shape[1]
    tm = _tile(rows_per_mod, 1024)
    per = rows_per_mod // tm
    x_spec = pl.BlockSpec((tm, d), lambda i, j: (i, 0))
    g_spec = pl.BlockSpec((1, d), lambda i, j: (0, 0))
    mod_spec = pl.BlockSpec((None, 1, d), lambda i, j: (i // per, 0, 0))
    scratch = [pltpu.VMEM((tm, d), BF16)]
    if glu_bias is None:
        tn = _tile(n, 1024)
        return pl.pallas_call(
            _nmm_kernel,
            grid=(m // tm, n // tn),
            in_specs=[x_spec, g_spec, mod_spec, mod_spec,
                      pl.BlockSpec((d, tn), lambda i, j: (0, j))],
            out_specs=pl.BlockSpec((tm, tn), lambda i, j: (i, j)),
            out_shape=jax.ShapeDtypeStruct((m, n), out_dtype),
            scratch_shapes=scratch,
            compiler_params=_params(("parallel", "arbitrary")),
            name="norm_mod_matmul",
        )(x2, g, sh, sc, w)
    nh = n // 2
    tn = _tile(nh, 1024)
    nb = nh // tn
    return pl.pallas_call(
        _nmm_glu_kernel,
        grid=(m // tm, nb),
        in_specs=[x_spec, g_spec, mod_spec, mod_spec,
                  pl.BlockSpec((d, tn), lambda i, j: (0, j)),
                  pl.BlockSpec((d, tn), lambda i, j: (0, j + nb)),
                  pl.BlockSpec((1, tn), lambda i, j: (0, j)),
                  pl.BlockSpec((1, tn), lambda i, j: (0, j + nb))],
        out_specs=pl.BlockSpec((tm, tn), lambda i, j: (i, j)),
        out_shape=jax.ShapeDtypeStruct((m, nh), out_dtype),
        scratch_shapes=scratch,
        compiler_params=_params(("parallel", "arbitrary")),
        name="norm_mod_matmul_glu",
    )(x2, g, sh, sc, w, w, glu_bias, glu_bias)


@functools.lru_cache(maxsize=None)
def _scan_consts(c):
    nl = int(math.log2(c))
    assert 1 << nl == c
    idx = np.arange(c)
    mats = np.zeros((2, 2 + SCAN_MATMUL_LEVELS, c, c), np.float32)
    level = np.zeros((2, c, c), np.int32)
    for d in range(2):
        tau = idx if d == 0 else c - 1 - idx
        tt, tr = tau[:, None], tau[None, :]
        mats[d, 0] = tr <= tt
        mats[d, 1] = tr > tt
        for l in range(SCAN_MATMUL_LEVELS):
            half = 1 << l
            mid = (tau // (2 * half)) * (2 * half) + half - 1
            later = ((tau >> l) & 1) == 1
            mats[d, 2 + l] = ((later[:, None] & (tr > mid[:, None]) & (tr <= tt))
                              | ((~later)[:, None] & (tr > tt) & (tr <= mid[:, None])))
        xor = tt ^ tr
        lev = np.floor(np.log2(np.maximum(xor, 1))).astype(np.int32)
        level[d] = np.where(tt > tr, lev, np.where(tt == tr, -1, -2))
    return mats.reshape(2, (2 + SCAN_MATMUL_LEVELS) * c, c), level, nl


def _scan_direction(d, zf, v, q, lb, est_ref, lev_ref, st_ref, o_ref, n_levels):
    c = zf.shape[0]
    f = lb + (1.0 - lb) * jax.nn.sigmoid(zf)
    logf = jnp.log(f)
    kk = (1.0 - lb) * jax.nn.sigmoid(-zf)
    lev = lev_ref[d]

    def block_diag(x):
        zero = jnp.zeros((x.shape[0], LANES), BF16)
        return jnp.concatenate([jnp.concatenate([x[:, :LANES], zero], axis=1),
                                jnp.concatenate([zero, x[:, LANES:]], axis=1)], axis=0)

    nt = (((1,), (1,)), ((), ()))
    for hp in range(st_ref.shape[0]):
        sl = slice(2 * hp * LANES, 2 * (hp + 1) * LANES)
        lf = logf[:, sl]
        hi = lf.astype(BF16)
        mid = (lf - hi.astype(F32)).astype(BF16)
        ex = jnp.dot(est_ref[d], jnp.concatenate([hi, mid], axis=0), preferred_element_type=F32)
        p = jnp.exp(ex)
        dec = jnp.exp(ex[0:1, :] + ex[c:c + 1, :])
        q_h, k_h, v_h = q[:, sl], kk[:, sl], v[:, sl]
        st = st_ref[hp]
        qb = (q_h * p[0:c]).astype(BF16)
        inter = lax.dot_general(qb, block_diag(st.astype(BF16)), nt, preferred_element_type=F32)
        pr = lax.dot_general(q_h.astype(BF16), block_diag(k_h.astype(BF16)), nt,
                             preferred_element_type=F32)
        scores = jnp.where(lev == -1, pr, 0.0)
        b = ex[0:c]
        for l in range(n_levels):
            if l < SCAN_MATMUL_LEVELS:
                pl_ = p[(2 + l) * c:(3 + l) * c]
            else:
                half = 1 << l
                rows = [jnp.broadcast_to(b[p0 + half - 1 + d:p0 + half + d, :], (2 * half, 2 * LANES))
                        for p0 in range(0, c, 2 * half)]
                bmid = jnp.concatenate(rows, axis=0) if len(rows) > 1 else rows[0]
                pl_ = jnp.exp(-jnp.abs(b - bmid))
            ql = (q_h * pl_).astype(BF16)
            kl = (k_h * pl_).astype(BF16)
            pr = lax.dot_general(ql, block_diag(kl), nt, preferred_element_type=F32)
            scores = jnp.where(lev == l, pr, scores)
        intra = jnp.dot(scores.astype(BF16), block_diag(v_h.astype(BF16)), preferred_element_type=F32)
        o_ref[:, sl] = inter + intra
        kend = (k_h * p[c:2 * c]).astype(BF16)
        vt = jnp.concatenate([v_h[:, :LANES].T, v_h[:, LANES:].T], axis=1).astype(BF16)
        st_ref[hp] = st * dec + jnp.dot(vt, block_diag(kend), preferred_element_type=F32)


def _scan_kernel(lbraw_ref, est_ref, lev_ref, s0f_ref, s0b_ref,
                 vf_ref, ff_ref, qf_ref, vb_ref, fb_ref, qb_ref,
                 of_ref, ob_ref, sf_ref, sb_ref, stf_scr, stb_scr, *, layer, n_levels):
    j = pl.program_id(1)

    @pl.when(j == 0)
    def _():
        stf_scr[...] = s0f_ref[...]
        stb_scr[...] = s0b_ref[...]

    raw = lbraw_ref[...]
    e = jnp.exp(raw - jnp.max(raw, axis=0, keepdims=True))
    lb_all = jnp.sum(e[0:layer + 1], axis=0, keepdims=True) / jnp.sum(e, axis=0, keepdims=True)
    d_model = vf_ref.shape[-1]
    _scan_direction(0, ff_ref[...], vf_ref[...], qf_ref[...], lb_all[:, :d_model],
                    est_ref, lev_ref, stf_scr, of_ref, n_levels)
    _scan_direction(1, fb_ref[...], vb_ref[...], qb_ref[...], lb_all[:, d_model:],
                    est_ref, lev_ref, stb_scr, ob_ref, n_levels)

    @pl.when(j == pl.num_programs(1) - 1)
    def _():
        sf_ref[...] = stf_scr[...]
        sb_ref[...] = stb_scr[...]


def _scan_call(z, hg_lb, s0f, s0b, layer):
    b, l, d5 = z.shape
    d = d5 // 5
    pairs = d // (2 * LANES)
    c = SCAN_CHUNK
    nc = l // c
    est, lev, n_levels = _scan_consts(c)
    est = jnp.asarray(np.concatenate([est, est], axis=2), BF16)
    lev = jnp.asarray(np.concatenate([lev, lev], axis=2))

    def zspec(col, rev):
        if rev:
            return pl.BlockSpec((None, c, d), lambda bi, j: (bi, nc - 1 - j, col))
        return pl.BlockSpec((None, c, d), lambda bi, j: (bi, j, col))

    full = lambda a: pl.BlockSpec(a.shape, lambda bi, j: (0,) * a.ndim)
    st_spec = pl.BlockSpec((None, pairs, LANES, 2 * LANES), lambda bi, j: (bi, 0, 0, 0))
    return pl.pallas_call(
        functools.partial(_scan_kernel, layer=layer, n_levels=n_levels),
        grid=(b, nc),
        in_specs=[full(hg_lb), full(est), full(lev), st_spec, st_spec,
                  zspec(0, False), zspec(1, False), zspec(3, False),
                  zspec(0, True), zspec(2, True), zspec(3, True)],
        out_specs=[pl.BlockSpec((None, c, d), lambda bi, j: (bi, j, 0)),
                   pl.BlockSpec((None, c, d), lambda bi, j: (bi, nc - 1 - j, 0)),
                   st_spec, st_spec],
        out_shape=[jax.ShapeDtypeStruct((b, l, d), F32), jax.ShapeDtypeStruct((b, l, d), F32),
                   jax.ShapeDtypeStruct(s0f.shape, F32), jax.ShapeDtypeStruct(s0b.shape, F32)],
        scratch_shapes=[pltpu.VMEM((pairs, LANES, 2 * LANES), F32),
                        pltpu.VMEM((pairs, LANES, 2 * LANES), F32)],
        compiler_params=_params(("parallel", "arbitrary")),
        name="hgrn2_scan",
    )(hg_lb, est, lev, s0f, s0b, z, z, z, z, z, z)


def _mix_out_kernel(of_ref, ob_ref, gate_ref, gn_ref, w_ref, x_ref, gt_ref, o_ref, h_scr):
    @pl.when(pl.program_id(1) == 0)
    def _():
        o = of_ref[...] + ob_ref[...]
        gate = gate_ref[...]
        gn = gn_ref[...]
        for h in range(o.shape[1] // LANES):
            sl = slice(h * LANES, (h + 1) * LANES)
            oh = o[:, sl]
            y = oh * lax.rsqrt(jnp.mean(oh * oh, axis=-1, keepdims=True) + EPS) * gn
            gh = gate[:, sl]
            h_scr[:, sl] = (y * (gh * jax.nn.sigmoid(gh))).astype(BF16)

    mix = jnp.dot(h_scr[...], w_ref[...], preferred_element_type=F32)
    o_ref[...] = x_ref[...] + gt_ref[...] * mix


def _mix_out_call(o_f, o_b, z2, gnorm, w_out, x2, gt, rows_per_mod):
    m, d = x2.shape
    tm = _tile(rows_per_mod, 1024)
    per = rows_per_mod // tm
    tn = _tile(d, 1024)
    row = pl.BlockSpec((tm, d), lambda i, j: (i, 0))
    return pl.pallas_call(
        _mix_out_kernel,
        grid=(m // tm, d // tn),
        in_specs=[row, row,
                  pl.BlockSpec((tm, d), lambda i, j: (i, 4)),
                  pl.BlockSpec((1, LANES), lambda i, j: (0, 0)),
                  pl.BlockSpec((d, tn), lambda i, j: (0, j)),
                  pl.BlockSpec((tm, tn), lambda i, j: (i, j)),
                  pl.BlockSpec((None, 1, tn), lambda i, j: (i // per, 0, j))],
        out_specs=pl.BlockSpec((tm, tn), lambda i, j: (i, j)),
        out_shape=jax.ShapeDtypeStruct((m, d), F32),
        scratch_shapes=[pltpu.VMEM((tm, d), BF16)],
        compiler_params=_params(("parallel", "arbitrary")),
        name="hgrn2_out",
    )(o_f, o_b, z2, gnorm, w_out, x2, gt)


def _conv_kernel(y_ref, w_ref, o_ref, padw_scr, padh_scr, *, rows, n_w_tiles):
    ct = pl.program_id(1)
    half = (CONV_WIDTH - 1) // 2
    l, lanes = y_ref.shape
    lead = padw_scr.shape[1] - GRID_W - 16

    @pl.when(ct < n_w_tiles)
    def _():
        padw_scr[...] = jnp.zeros(padw_scr.shape, F32)
        padw_scr[:, lead:lead + GRID_W, :] = y_ref[...].reshape(rows, GRID_W, lanes)
        acc = jnp.zeros((rows, GRID_W, lanes), F32)
        for k in range(CONV_WIDTH):
            start = lead + k - half
            acc = acc + w_ref[k:k + 1, :].reshape(1, 1, lanes) * padw_scr[:, start:start + GRID_W, :]
        o_ref[...] = acc.reshape(l, lanes)

    @pl.when(ct >= n_w_tiles)
    def _():
        margin = half * GRID_W
        padh_scr[...] = jnp.zeros(padh_scr.shape, F32)
        padh_scr[margin:margin + l, :] = y_ref[...]
        acc = jnp.zeros((l, lanes), F32)
        for k in range(CONV_WIDTH):
            acc = acc + w_ref[k:k + 1, :] * padh_scr[k * GRID_W:k * GRID_W + l, :]
        o_ref[...] = acc


def _conv_call(y, w_dw):
    b, l, c = y.shape
    rows = l // GRID_W
    half = (CONV_WIDTH - 1) // 2
    n_w_tiles = (c // 2) // LANES
    wp = jnp.pad(w_dw, ((0, 32 - CONV_WIDTH), (0, 0)))
    return pl.pallas_call(
        functools.partial(_conv_kernel, rows=rows, n_w_tiles=n_w_tiles),
        grid=(b, c // LANES),
        in_specs=[pl.BlockSpec((None, l, LANES), lambda bi, ct: (bi, 0, ct)),
                  pl.BlockSpec((32, LANES), lambda bi, ct: (0, ct))],
        out_specs=pl.BlockSpec((None, l, LANES), lambda bi, ct: (bi, 0, ct)),
        out_shape=jax.ShapeDtypeStruct((b, l, c), F32),
        scratch_shapes=[pltpu.VMEM((rows, GRID_W + 32, LANES), F32),
                        pltpu.VMEM((l + 2 * half * GRID_W, LANES), F32)],
        compiler_params=_params(("parallel", "parallel")),
        name="axial_dwconv",
    )(y, wp)


def _ln_out_kernel(y_ref, bdw_ref, lg_ref, lb_ref, w_ref, b2_ref, x_ref, gt_ref, o_ref, h_scr):
    @pl.when(pl.program_id(1) == 0)
    def _():
        y = y_ref[...] + bdw_ref[...]
        yc = y - jnp.mean(y, axis=-1, keepdims=True)
        yn = yc * lax.rsqrt(jnp.mean(yc * yc, axis=-1, keepdims=True) + EPS)
        yn = yn * lg_ref[...] + lb_ref[...]
        h_scr[...] = (yn * jax.nn.sigmoid(yn)).astype(BF16)

    mix = jnp.dot(h_scr[...], w_ref[...], preferred_element_type=F32) + b2_ref[...]
    o_ref[...] = x_ref[...] + gt_ref[...] * mix


def _ln_out_call(y2, b_dw, ln_g, ln_b, w_pw2, b_pw2, x2, gt, rows_per_mod):
    m, d = x2.shape
    tm = _tile(rows_per_mod, 1024)
    per = rows_per_mod // tm
    tn = _tile(d, 1024)
    vec = pl.BlockSpec((1, d), lambda i, j: (0, 0))
    return pl.pallas_call(
        _ln_out_kernel,
        grid=(m // tm, d // tn),
        in_specs=[pl.BlockSpec((tm, d), lambda i, j: (i, 0)), vec, vec, vec,
                  pl.BlockSpec((d, tn), lambda i, j: (0, j)),
                  pl.BlockSpec((1, tn), lambda i, j: (0, j)),
                  pl.BlockSpec((tm, tn), lambda i, j: (i, j)),
                  pl.BlockSpec((None, 1, tn), lambda i, j: (i // per, 0, j))],
        out_specs=pl.BlockSpec((tm, tn), lambda i, j: (i, j)),
        out_shape=jax.ShapeDtypeStruct((m, d), F32),
        scratch_shapes=[pltpu.VMEM((tm, d), BF16)],
        compiler_params=_params(("parallel", "arbitrary")),
        name="conv_out",
    )(y2, b_dw, ln_g, ln_b, w_pw2, b_pw2, x2, gt)


def _vmax(a, b):
    if a is None:
        return b
    if b is None:
        return a
    return jnp.maximum(a, b)


def _vmin(a, b):
    if a is None or b is None:
        return None
    return jnp.minimum(a, b)


def _bitonic_merge_desc(xs):
    n = len(xs)
    j = n // 2
    while j >= 1:
        for i in range(n):
            o = i ^ j
            if o > i:
                a, b = xs[i], xs[o]
                xs[i], xs[o] = _vmax(a, b), _vmin(a, b)
        j //= 2
    return xs


def _bitonic_sort_desc(xs):
    n = len(xs)
    k = 2
    while k <= n:
        j = k // 2
        while j >= 1:
            for i in range(n):
                o = i ^ j
                if o > i:
                    a, b = xs[i], xs[o]
                    if (i & k) == 0:
                        xs[i], xs[o] = _vmax(a, b), _vmin(a, b)
                    else:
                        xs[i], xs[o] = _vmin(a, b), _vmax(a, b)
            j //= 2
        k *= 2
    return xs


def _merge_top(a, b):
    n = len(a)
    b = list(b) + [None] * (n - len(b))
    return _bitonic_merge_desc([_vmax(a[i], b[n - 1 - i]) for i in range(n)])


def _topk_kernel(q_ref, keys_ref, th_ref, e1_ref, s2_ref, e2_ref, s_scr):
    heads = th_ref.shape[0]
    nk = keys_ref.shape[1]
    k = PEER_TOPK
    t = q_ref.shape[0]
    sub = lax.broadcasted_iota(jnp.int32, (SUBLANES, t), 0)
    tops = [[None] * k, [None] * k]
    for h in range(heads):
        for p in range(2):
            hp = 2 * h + p
            dk = keys_ref.shape[2]
            s = lax.dot_general(keys_ref[hp], q_ref[:, hp * dk:(hp + 1) * dk],
                                (((1,), (1,)), ((), ())), preferred_element_type=F32)
            s_scr[hp] = s
            groups = _bitonic_sort_desc([s[SUBLANES * a:SUBLANES * (a + 1), :] for a in range(nk // SUBLANES)])
            assert len(groups) == k
            for shift in (4, 2, 1):
                groups = _merge_top(groups, [pltpu.roll(g, shift, 0) for g in groups])
            for a in range(k):
                prev = tops[p][a]
                tops[p][a] = groups[a] if prev is None else jnp.where(sub == h, groups[a], prev)
    v1, v2 = tops
    best = [v1[0] + v2[b] for b in range(k)]
    for a in range(1, k // 2):
        best = _merge_top(best, [v1[a] + v2[b] for b in range(k // (a + 1))])
    best = _merge_top(best, [v1[a] + v2[0] for a in range(k // 2, k)])
    cmax = best[0]
    zsum = jnp.ones_like(cmax)
    for a in range(1, k):
        zsum = zsum + jnp.exp(best[a] - cmax)
    tau = best[k - 1]
    inv_z = 1.0 / zsum
    for h in range(heads):
        s1 = s_scr[2 * h]
        s2 = s_scr[2 * h + 1]
        tau_h = tau[h:h + 1, :]
        slack = (jnp.abs(tau_h) + jnp.abs(s1)) * (2.0 ** -22)
        th_ref[h] = (tau_h - s1) - slack
        e1_ref[h] = jnp.exp(s1 - v1[0][h:h + 1, :])
        s2_ref[h] = s2
        e2_ref[h] = jnp.exp(s2 - v2[0][h:h + 1, :]) * inv_z[h:h + 1, :]


def _topk_call(qk, keys):
    m = qk.shape[0]
    hp, nk, dk = keys.shape
    heads = hp // 2
    t = _tile(m, 512)
    out = jax.ShapeDtypeStruct((heads, nk, m), F32)
    ospec = pl.BlockSpec((heads, nk, t), lambda i: (0, 0, i))
    return pl.pallas_call(
        _topk_kernel,
        grid=(m // t,),
        in_specs=[pl.BlockSpec((t, hp * dk), lambda i: (i, 0)),
                  pl.BlockSpec((hp, nk, dk), lambda i: (0, 0, 0))],
        out_specs=[ospec, ospec, ospec, ospec],
        out_shape=[out, out, out, out],
        scratch_shapes=[pltpu.VMEM((hp, nk, t), F32)],
        compiler_params=_params(("parallel",)),
        name="peer_topk",
    )(qk, keys)


def _peer_weights(i, a, g_out, row0, bc_scr, th_ref, e1_ref, s2_ref, e2_ref):
    heads, nk, t = th_ref.shape
    for h in range(heads):
        bc_scr[0, h] = jnp.broadcast_to(th_ref[h, pl.ds(i, 1), :], (SUBLANES, t))
        bc_scr[1, h] = jnp.broadcast_to(e1_ref[h, pl.ds(i, 1), :], (SUBLANES, t))
    for tl in range(t // LANES):
        lanes = slice(tl * LANES, (tl + 1) * LANES)
        w = None
        for h in range(heads):
            th = jnp.concatenate([bc_scr[0, h, :, lanes]] * (nk // SUBLANES), axis=0)
            e1 = jnp.concatenate([bc_scr[1, h, :, lanes]] * (nk // SUBLANES), axis=0)
            term = jnp.where(s2_ref[h, :, lanes] >= th, e2_ref[h, :, lanes], 0.0) * e1
            w = term if w is None else w + term
        a_t = a[:, lanes]
        act = 0.5 * a_t * (1.0 + lax.erf(a_t * (1.0 / math.sqrt(2.0))))
        g_out[row0:row0 + nk, lanes] = (w * act).astype(BF16)


def _peer_kernel(x_ref, g_ref, sh_ref, sc_ref, gt_ref, th_ref, e1_ref, s2_ref, e2_ref,
                 u_ref, vtp_ref, vtc_ref, vtl_ref, o_ref, ht_scr, ga_scr, gb_scr, acc_scr, bc_scr):
    kb = pl.program_id(1)
    nk = th_ref.shape[1]
    n_sub = PEER_EB // PEER_SUB
    sel = (th_ref, e1_ref, s2_ref, e2_ref)

    @pl.when(kb == 0)
    def _():
        h = _rmsnorm_mod(x_ref[...], g_ref[...], sh_ref[...], sc_ref[...])
        ht_scr[...] = h.T.astype(BF16)
        gb_scr[...] = jnp.zeros(gb_scr.shape, BF16)
        acc_scr[...] = jnp.zeros(acc_scr.shape, F32)

    def first_matmul(idx):
        rows = slice(idx * PEER_SUB, (idx + 1) * PEER_SUB)
        return jnp.dot(u_ref[rows, :], ht_scr[...], preferred_element_type=F32)

    a_next = first_matmul(0)
    for idx in range(2 * n_sub):
        phase, sb = divmod(idx, n_sub)
        a = a_next
        if idx + 1 < 2 * n_sub:
            a_next = first_matmul(idx + 1)
        if sb == min(1, n_sub - 1):
            vt_ref, g_in = (vtp_ref, gb_scr) if phase == 0 else (vtc_ref, ga_scr)
            acc_scr[...] += jnp.dot(vt_ref[...], g_in[...], preferred_element_type=F32)
        g_out = ga_scr if phase == 0 else gb_scr
        for gi in range(PEER_SUB // nk):
            i = kb * (2 * PEER_EB // nk) + idx * (PEER_SUB // nk) + gi
            _peer_weights(i, a[gi * nk:(gi + 1) * nk, :], g_out, sb * PEER_SUB + gi * nk, bc_scr, *sel)

    @pl.when(kb == pl.num_programs(1) - 1)
    def _():
        acc = acc_scr[...] + jnp.dot(vtl_ref[...], gb_scr[...], preferred_element_type=F32)
        o_ref[...] = x_ref[...] + gt_ref[...] * acc.T


def _peer_call(x2, g, sh, sc, gt, th, e1, s2, e2, u, vt, rows_per_mod):
    m, d = x2.shape
    heads, nk, _ = th.shape
    e = u.shape[0]
    t = _tile(rows_per_mod, 512)
    per = rows_per_mod // t
    eb = PEER_EB
    nkb = e // (2 * eb)
    assert nkb * 2 * eb == e
    sel = pl.BlockSpec((heads, nk, t), lambda i, k: (0, 0, i))
    mod = pl.BlockSpec((None, 1, d), lambda i, k: (i // per, 0, 0))
    return pl.pallas_call(
        _peer_kernel,
        grid=(m // t, nkb),
        in_specs=[pl.BlockSpec((t, d), lambda i, k: (i, 0)),
                  pl.BlockSpec((1, d), lambda i, k: (0, 0)),
                  mod, mod, mod, sel, sel, sel, sel,
                  pl.BlockSpec((2 * eb, d), lambda i, k: (k, 0)),
                  pl.BlockSpec((d, eb), lambda i, k: (0, jnp.maximum(2 * k - 1, 0))),
                  pl.BlockSpec((d, eb), lambda i, k: (0, 2 * k)),
                  pl.BlockSpec((d, eb), lambda i, k: (0, 2 * nkb - 1))],
        out_specs=pl.BlockSpec((t, d), lambda i, k: (i, 0)),
        out_shape=jax.ShapeDtypeStruct((m, d), F32),
        scratch_shapes=[pltpu.VMEM((d, t), BF16), pltpu.VMEM((eb, t), BF16),
                        pltpu.VMEM((eb, t), BF16), pltpu.VMEM((d, t), F32),
                        pltpu.VMEM((2, heads, SUBLANES, t), F32)],
        compiler_params=_params(("parallel", "arbitrary")),
        name="peer_experts",
    )(x2, g, sh, sc, gt, th, e1, s2, e2, u, vt, vt, vt)


def _final_kernel(x_ref, g_ref, o_ref):
    x = x_ref[...]
    o_ref[...] = x * lax.rsqrt(jnp.mean(x * x, axis=-1, keepdims=True) + EPS) * g_ref[...]


def _final_call(x2, g):
    m, d = x2.shape
    tm = _tile(m, 1024)
    return pl.pallas_call(
        _final_kernel,
        grid=(m // tm,),
        in_specs=[pl.BlockSpec((tm, d), lambda i: (i, 0)), pl.BlockSpec((1, d), lambda i: (0, 0))],
        out_specs=pl.BlockSpec((tm, d), lambda i: (i, 0)),
        out_shape=jax.ShapeDtypeStruct((m, d), F32),
        compiler_params=_params(("parallel",)),
        name="final_norm",
    )(x2, g)


def _peer_layer(x2, g_ffn, sh, sc, gt, w_q, keys, u, v, seq):
    d = x2.shape[1]
    qk = _nmm_call(x2, g_ffn, sh, sc, w_q.astype(BF16), seq, BF16)
    kb = keys.reshape(2 * PEER_HEADS, PEER_NKEYS, PEER_DKEY // 2).astype(BF16)
    th, e1, s2, e2 = _topk_call(qk, kb)
    return _peer_call(x2, g_ffn, sh, sc, gt, th, e1, s2, e2,
                      u.astype(BF16), v.astype(BF16).T, seq)


def kernel(x, c, ctx, c_ctx, w_ada, b_ada, g_mix, g_ffn, hg_w_in, hg_w_out, hg_gnorm, hg_lb,
           cv_w_pw1, cv_b_pw1, cv_w_dw, cv_b_dw, cv_ln_g, cv_ln_b, cv_w_pw2, cv_b_pw2,
           peer_w_q, peer_keys, peer_u, peer_v, g_final):
    b, seq, d = x.shape
    ctx_len = ctx.shape[1]
    depth = w_ada.shape[0]
    heads = d // LANES
    assert heads * LANES == d and seq % SCAN_CHUNK == 0 and ctx_len % SCAN_CHUNK == 0

    pad_rows = (-(b + 1)) % SUBLANES
    cc = jnp.concatenate([c, c_ctx[None, :], jnp.zeros((pad_rows, d), F32)], axis=0)
    ada = _ada_call(cc, w_ada, b_ada)

    x2 = x.reshape(b * seq, d)
    for l in range(depth):
        mixer = l % N_MIXERS
        mods = ada[l, :b].reshape(b, 1, 6, d)
        sh_m, sc_m, gt_m, sh_f, sc_f, gt_f = [mods[:, :, i, :] for i in range(6)]
        g_mix_l = g_mix[l][None, :]
        g_ffn_l = g_ffn[l][None, :]
        if mixer == 0:
            a = l // N_MIXERS
            w_in = hg_w_in[a].astype(BF16)
            cmods = ada[l, b].reshape(6, d)
            csh = jnp.broadcast_to(cmods[0][None, None, :], (b, 1, d))
            csc = jnp.broadcast_to(cmods[1][None, None, :], (b, 1, d))
            zc = _nmm_call(ctx.reshape(b * ctx_len, d), g_mix_l, csh, csc, w_in, ctx_len, F32)
            s0 = jnp.zeros((b, heads // 2, LANES, 2 * LANES), F32)
            _, _, s_f, s_b = _scan_call(zc.reshape(b, ctx_len, 5 * d), hg_lb, s0, s0, l)
            z = _nmm_call(x2, g_mix_l, sh_m, sc_m, w_in, seq, F32)
            o_f, o_b, _, _ = _scan_call(z.reshape(b, seq, 5 * d), hg_lb, s_f, s_b, l)
            x2 = _mix_out_call(o_f.reshape(b * seq, d), o_b.reshape(b * seq, d), z,
                               hg_gnorm[a][None, :], hg_w_out[a].astype(BF16), x2, gt_m, seq)
        else:
            bi = l // N_MIXERS
            y = _nmm_call(x2, g_mix_l, sh_m, sc_m, cv_w_pw1[bi].astype(BF16), seq, F32,
                          glu_bias=cv_b_pw1[bi][None, :])
            y = _conv_call(y.reshape(b, seq, d), cv_w_dw[bi])
            x2 = _ln_out_call(y.reshape(b * seq, d), cv_b_dw[bi][None, :], cv_ln_g[bi][None, :],
                              cv_ln_b[bi][None, :], cv_w_pw2[bi].astype(BF16),
                              cv_b_pw2[bi][None, :], x2, gt_m, seq)
        x2 = _peer_layer(x2, g_ffn_l, sh_f, sc_f, gt_f, peer_w_q[l], peer_keys[l],
                         peer_u[l], peer_v[l], seq)
    return _final_call(x2, g_final[None, :]).reshape(b, seq, d)
```

```python
import functools
import math

import numpy as np
import jax
import jax.numpy as jnp
from jax import lax
from jax.experimental import pallas as pl
from jax.experimental.pallas import tpu as pltpu

EPS = 1e-6
GRID_W = 64
CONV_WIDTH = 31
HG_HEADS = 8
PEER_HEADS = 8
PEER_NKEYS = 128
PEER_DKEY = 256
PEER_TOPK = 16
N_MIXERS = 2

LANES = 128
SUBLANES = 8
SCAN_CHUNK = 128
SCAN_MATMUL_LEVELS = int(math.log2(SUBLANES))
PEER_EB = 512
PEER_SUB = 256
MXU_WIDTH = 256
VMEM_LIMIT = 56 * 1024 * 1024

F32 = jnp.float32
BF16 = jnp.bfloat16


def _tile(n, pref):
    t = min(n, pref)
    while n % t:
        t //= 2
    return t


def _params(sem):
    return pltpu.CompilerParams(dimension_semantics=sem, vmem_limit_bytes=VMEM_LIMIT)


def _rmsnorm_mod(x, g, sh, sc):
    y = x * lax.rsqrt(jnp.mean(x * x, axis=-1, keepdims=True) + EPS) * g
    return y * (1.0 + sc) + sh


def _ada_kernel(c_ref, w_ref, b_ref, o_ref):
    s = c_ref[...]
    s = s * jax.nn.sigmoid(s)
    o_ref[...] = jnp.dot(s, w_ref[...], preferred_element_type=F32,
                         precision=lax.Precision.HIGHEST) + b_ref[...]


def _ada_call(cc, w_ada, b_ada):
    depth, d, n = w_ada.shape
    r = cc.shape[0]
    tn = _tile(n, 1024)
    return pl.pallas_call(
        _ada_kernel,
        grid=(depth, n // tn),
        in_specs=[pl.BlockSpec((r, d), lambda l, j: (0, 0)),
                  pl.BlockSpec((None, d, tn), lambda l, j: (l, 0, j)),
                  pl.BlockSpec((None, 1, tn), lambda l, j: (l, 0, j))],
        out_specs=pl.BlockSpec((None, r, tn), lambda l, j: (l, 0, j)),
        out_shape=jax.ShapeDtypeStruct((depth, r, n), F32),
        compiler_params=_params(("parallel", "parallel")),
        name="ada",
    )(cc, w_ada, b_ada.reshape(depth, 1, n))


def _nmm_kernel(x_ref, g_ref, sh_ref, sc_ref, w_ref, o_ref, h_scr):
    @pl.when(pl.program_id(1) == 0)
    def _():
        h_scr[...] = _rmsnorm_mod(x_ref[...], g_ref[...], sh_ref[...], sc_ref[...]).astype(BF16)

    o_ref[...] = jnp.dot(h_scr[...], w_ref[...], preferred_element_type=F32).astype(o_ref.dtype)


def _nmm_glu_kernel(x_ref, g_ref, sh_ref, sc_ref, wa_ref, wg_ref, ba_ref, bg_ref, o_ref, h_scr):
    @pl.when(pl.program_id(1) == 0)
    def _():
        h_scr[...] = _rmsnorm_mod(x_ref[...], g_ref[...], sh_ref[...], sc_ref[...]).astype(BF16)

    h = h_scr[...]
    a = jnp.dot(h, wa_ref[...], preferred_element_type=F32) + ba_ref[...]
    g = jnp.dot(h, wg_ref[...], preferred_element_type=F32) + bg_ref[...]
    o_ref[...] = (a * jax.nn.sigmoid(g)).astype(o_ref.dtype)


def _nmm_call(x2, g, sh, sc, w, rows_per_mod, out_dtype, glu_bias=None):
    m, d = x2.shape
    n = w.shape[1]
    tm = _tile(rows_per_mod, 1024)
    per = rows_per_mod // tm
    x_spec = pl.BlockSpec((tm, d), lambda i, j: (i, 0))
    g_spec = pl.BlockSpec((1, d), lambda i, j: (0, 0))
    mod_spec = pl.BlockSpec((None, 1, d), lambda i, j: (i // per, 0, 0))
    scratch = [pltpu.VMEM((tm, d), BF16)]
    if glu_bias is None:
        tn = _tile(n, 1024)
        return pl.pallas_call(
            _nmm_kernel,
            grid=(m // tm, n // tn),
            in_specs=[x_spec, g_spec, mod_spec, mod_spec,
                      pl.BlockSpec((d, tn), lambda i, j: (0, j))],
            out_specs=pl.BlockSpec((tm, tn), lambda i, j: (i, j)),
            out_shape=jax.ShapeDtypeStruct((m, n), out_dtype),
            scratch_shapes=scratch,
            compiler_params=_params(("parallel", "arbitrary")),
            name="norm_mod_matmul",
        )(x2, g, sh, sc, w)
    nh = n // 2
    tn = _tile(nh, 1024)
    nb = nh // tn
    return pl.pallas_call(
        _nmm_glu_kernel,
        grid=(m // tm, nb),
        in_specs=[x_spec, g_spec, mod_spec, mod_spec,
                  pl.BlockSpec((d, tn), lambda i, j: (0, j)),
                  pl.BlockSpec((d, tn), lambda i, j: (0, j + nb)),
                  pl.BlockSpec((1, tn), lambda i, j: (0, j)),
                  pl.BlockSpec((1, tn), lambda i, j: (0, j + nb))],
        out_specs=pl.BlockSpec((tm, tn), lambda i, j: (i, j)),
        out_shape=jax.ShapeDtypeStruct((m, nh), out_dtype),
        scratch_shapes=scratch,
        compiler_params=_params(("parallel", "arbitrary")),
        name="norm_mod_matmul_glu",
    )(x2, g, sh, sc, w, w, glu_bias, glu_bias)


@functools.lru_cache(maxsize=None)
def _scan_consts(c):
    nl = int(math.log2(c))
    assert 1 << nl == c
    idx = np.arange(c)
    mats = np.zeros((2, 2 + SCAN_MATMUL_LEVELS, c, c), np.float32)
    level = np.zeros((2, c, c), np.int32)
    for d in range(2):
        tau = idx if d == 0 else c - 1 - idx
        tt, tr = tau[:, None], tau[None, :]
        mats[d, 0] = tr <= tt
        mats[d, 1] = tr > tt
        for l in range(SCAN_MATMUL_LEVELS):
            half = 1 << l
            mid = (tau // (2 * half)) * (2 * half) + half - 1
            later = ((tau >> l) & 1) == 1
            mats[d, 2 + l] = ((later[:, None] & (tr > mid[:, None]) & (tr <= tt))
                              | ((~later)[:, None] & (tr > tt) & (tr <= mid[:, None])))
        xor = tt ^ tr
        lev = np.floor(np.log2(np.maximum(xor, 1))).astype(np.int32)
        level[d] = np.where(tt > tr, lev, np.where(tt == tr, -1, -2))
    return mats.reshape(2, (2 + SCAN_MATMUL_LEVELS) * c, c), level, nl


def _scan_direction(d, zf, v, q, lb, est_ref, lev_ref, st_ref, o_ref, n_levels):
    c = zf.shape[0]
    f = lb + (1.0 - lb) * jax.nn.sigmoid(zf)
    logf = jnp.log(f)
    kk = (1.0 - lb) * jax.nn.sigmoid(-zf)
    lev = lev_ref[d]

    def block_diag(x):
        zero = jnp.zeros((x.shape[0], LANES), BF16)
        return jnp.concatenate([jnp.concatenate([x[:, :LANES], zero], axis=1),
                                jnp.concatenate([zero, x[:, LANES:]], axis=1)], axis=0)

    nt = (((1,), (1,)), ((), ()))
    for hp in range(st_ref.shape[0]):
        sl = slice(2 * hp * LANES, 2 * (hp + 1) * LANES)
        lf = logf[:, sl]
        hi = lf.astype(BF16)
        mid = (lf - hi.astype(F32)).astype(BF16)
        ex = jnp.dot(est_ref[d], jnp.concatenate([hi, mid], axis=0), preferred_element_type=F32)
        p = jnp.exp(ex)
        dec = jnp.exp(ex[0:1, :] + ex[c:c + 1, :])
        q_h, k_h, v_h = q[:, sl], kk[:, sl], v[:, sl]
        st = st_ref[hp]
        qb = (q_h * p[0:c]).astype(BF16)
        inter = lax.dot_general(qb, block_diag(st.astype(BF16)), nt, preferred_element_type=F32)
        pr = lax.dot_general(q_h.astype(BF16), block_diag(k_h.astype(BF16)), nt,
                             preferred_element_type=F32)
        scores = jnp.where(lev == -1, pr, 0.0)
        b = ex[0:c]
        for l in range(n_levels):
            if l < SCAN_MATMUL_LEVELS:
                pl_ = p[(2 + l) * c:(3 + l) * c]
            else:
                half = 1 << l
                rows = [jnp.broadcast_to(b[p0 + half - 1 + d:p0 + half + d, :], (2 * half, 2 * LANES))
                        for p0 in range(0, c, 2 * half)]
                bmid = jnp.concatenate(rows, axis=0) if len(rows) > 1 else rows[0]
                pl_ = jnp.exp(-jnp.abs(b - bmid))
            ql = (q_h * pl_).astype(BF16)
            kl = (k_h * pl_).astype(BF16)
            pr = lax.dot_general(ql, block_diag(kl), nt, preferred_element_type=F32)
            scores = jnp.where(lev == l, pr, scores)
        intra = jnp.dot(scores.astype(BF16), block_diag(v_h.astype(BF16)), preferred_element_type=F32)
        o_ref[:, sl] = inter + intra
        kend = (k_h * p[c:2 * c]).astype(BF16)
        vt = jnp.concatenate([v_h[:, :LANES].T, v_h[:, LANES:].T], axis=1).astype(BF16)
        st_ref[hp] = st * dec + jnp.dot(vt, block_diag(kend), preferred_element_type=F32)


def _scan_kernel(lbraw_ref, est_ref, lev_ref, s0f_ref, s0b_ref,
                 vf_ref, ff_ref, qf_ref, vb_ref, fb_ref, qb_ref,
                 of_ref, ob_ref, sf_ref, sb_ref, stf_scr, stb_scr, *, layer, n_levels):
    j = pl.program_id(1)

    @pl.when(j == 0)
    def _():
        stf_scr[...] = s0f_ref[...]
        stb_scr[...] = s0b_ref[...]

    raw = lbraw_ref[...]
    e = jnp.exp(raw - jnp.max(raw, axis=0, keepdims=True))
    lb_all = jnp.sum(e[0:layer + 1], axis=0, keepdims=True) / jnp.sum(e, axis=0, keepdims=True)
    d_model = vf_ref.shape[-1]
    _scan_direction(0, ff_ref[...], vf_ref[...], qf_ref[...], lb_all[:, :d_model],
                    est_ref, lev_ref, stf_scr, of_ref, n_levels)
    _scan_direction(1, fb_ref[...], vb_ref[...], qb_ref[...], lb_all[:, d_model:],
                    est_ref, lev_ref, stb_scr, ob_ref, n_levels)

    @pl.when(j == pl.num_programs(1) - 1)
    def _():
        sf_ref[...] = stf_scr[...]
        sb_ref[...] = stb_scr[...]


def _scan_call(z, hg_lb, s0f, s0b, layer):
    b, l, d5 = z.shape
    d = d5 // 5
    pairs = d // (2 * LANES)
    c = SCAN_CHUNK
    nc = l // c
    est, lev, n_levels = _scan_consts(c)
    est = jnp.asarray(np.concatenate([est, est], axis=2), BF16)
    lev = jnp.asarray(np.concatenate([lev, lev], axis=2))

    def zspec(col, rev):
        if rev:
            return pl.BlockSpec((None, c, d), lambda bi, j: (bi, nc - 1 - j, col))
        return pl.BlockSpec((None, c, d), lambda bi, j: (bi, j, col))

    full = lambda a: pl.BlockSpec(a.shape, lambda bi, j: (0,) * a.ndim)
    st_spec = pl.BlockSpec((None, pairs, LANES, 2 * LANES), lambda bi, j: (bi, 0, 0, 0))
    return pl.pallas_call(
        functools.partial(_scan_kernel, layer=layer, n_levels=n_levels),
        grid=(b, nc),
        in_specs=[full(hg_lb), full(est), full(lev), st_spec, st_spec,
                  zspec(0, False), zspec(1, False), zspec(3, False),
                  zspec(0, True), zspec(2, True), zspec(3, True)],
        out_specs=[pl.BlockSpec((None, c, d), lambda bi, j: (bi, j, 0)),
                   pl.BlockSpec((None, c, d), lambda bi, j: (bi, nc - 1 - j, 0)),
                   st_spec, st_spec],
        out_shape=[jax.ShapeDtypeStruct((b, l, d), F32), jax.ShapeDtypeStruct((b, l, d), F32),
                   jax.ShapeDtypeStruct(s0f.shape, F32), jax.ShapeDtypeStruct(s0b.shape, F32)],
        scratch_shapes=[pltpu.VMEM((pairs, LANES, 2 * LANES), F32),
                        pltpu.VMEM((pairs, LANES, 2 * LANES), F32)],
        compiler_params=_params(("parallel", "arbitrary")),
        name="hgrn2_scan",
    )(hg_lb, est, lev, s0f, s0b, z, z, z, z, z, z)


def _mix_out_kernel(of_ref, ob_ref, gate_ref, gn_ref, w_ref, x_ref, gt_ref, o_ref, h_scr):
    @pl.when(pl.program_id(1) == 0)
    def _():
        o = of_ref[...] + ob_ref[...]
        gate = gate_ref[...]
        gn = gn_ref[...]
        for h in range(o.shape[1] // LANES):
            sl = slice(h * LANES, (h + 1) * LANES)
            oh = o[:, sl]
            y = oh * lax.rsqrt(jnp.mean(oh * oh, axis=-1, keepdims=True) + EPS) * gn
            gh = gate[:, sl]
            h_scr[:, sl] = (y * (gh * jax.nn.sigmoid(gh))).astype(BF16)

    mix = jnp.dot(h_scr[...], w_ref[...], preferred_element_type=F32)
    o_ref[...] = x_ref[...] + gt_ref[...] * mix


def _mix_out_call(o_f, o_b, z2, gnorm, w_out, x2, gt, rows_per_mod):
    m, d = x2.shape
    tm = _tile(rows_per_mod, 1024)
    per = rows_per_mod // tm
    tn = _tile(d, 1024)
    row = pl.BlockSpec((tm, d), lambda i, j: (i, 0))
    return pl.pallas_call(
        _mix_out_kernel,
        grid=(m // tm, d // tn),
        in_specs=[row, row,
                  pl.BlockSpec((tm, d), lambda i, j: (i, 4)),
                  pl.BlockSpec((1, LANES), lambda i, j: (0, 0)),
                  pl.BlockSpec((d, tn), lambda i, j: (0, j)),
                  pl.BlockSpec((tm, tn), lambda i, j: (i, j)),
                  pl.BlockSpec((None, 1, tn), lambda i, j: (i // per, 0, j))],
        out_specs=pl.BlockSpec((tm, tn), lambda i, j: (i, j)),
        out_shape=jax.ShapeDtypeStruct((m, d), F32),
        scratch_shapes=[pltpu.VMEM((tm, d), BF16)],
        compiler_params=_params(("parallel", "arbitrary")),
        name="hgrn2_out",
    )(o_f, o_b, z2, gnorm, w_out, x2, gt)


def _conv_kernel(y_ref, w_ref, o_ref, padw_scr, padh_scr, *, rows, n_w_tiles):
    ct = pl.program_id(1)
    half = (CONV_WIDTH - 1) // 2
    l, lanes = y_ref.shape
    lead = padw_scr.shape[1] - GRID_W - 16

    @pl.when(ct < n_w_tiles)
    def _():
        padw_scr[...] = jnp.zeros(padw_scr.shape, F32)
        padw_scr[:, lead:lead + GRID_W, :] = y_ref[...].reshape(rows, GRID_W, lanes)
        acc = jnp.zeros((rows, GRID_W, lanes), F32)
        for k in range(CONV_WIDTH):
            start = lead + k - half
            acc = acc + w_ref[k:k + 1, :].reshape(1, 1, lanes) * padw_scr[:, start:start + GRID_W, :]
        o_ref[...] = acc.reshape(l, lanes)

    @pl.when(ct >= n_w_tiles)
    def _():
        margin = half * GRID_W
        padh_scr[...] = jnp.zeros(padh_scr.shape, F32)
        padh_scr[margin:margin + l, :] = y_ref[...]
        acc = jnp.zeros((l, lanes), F32)
        for k in range(CONV_WIDTH):
            acc = acc + w_ref[k:k + 1, :] * padh_scr[k * GRID_W:k * GRID_W + l, :]
        o_ref[...] = acc


def _conv_call(y, w_dw):
    b, l, c = y.shape
    rows = l // GRID_W
    half = (CONV_WIDTH - 1) // 2
    n_w_tiles = (c // 2) // LANES
    wp = jnp.pad(w_dw, ((0, 32 - CONV_WIDTH), (0, 0)))
    return pl.pallas_call(
        functools.partial(_conv_kernel, rows=rows, n_w_tiles=n_w_tiles),
        grid=(b, c // LANES),
        in_specs=[pl.BlockSpec((None, l, LANES), lambda bi, ct: (bi, 0, ct)),
                  pl.BlockSpec((32, LANES), lambda bi, ct: (0, ct))],
        out_specs=pl.BlockSpec((None, l, LANES), lambda bi, ct: (bi, 0, ct)),
        out_shape=jax.ShapeDtypeStruct((b, l, c), F32),
        scratch_shapes=[pltpu.VMEM((rows, GRID_W + 32, LANES), F32),
                        pltpu.VMEM((l + 2 * half * GRID_W, LANES), F32)],
        compiler_params=_params(("parallel", "parallel")),
        name="axial_dwconv",
    )(y, wp)


def _ln_out_kernel(y_ref, bdw_ref, lg_ref, lb_ref, w_ref, b2_ref, x_ref, gt_ref, o_ref, h_scr):
    @pl.when(pl.program_id(1) == 0)
    def _():
        y = y_ref[...] + bdw_ref[...]
        yc = y - jnp.mean(y, axis=-1, keepdims=True)
        yn = yc * lax.rsqrt(jnp.mean(yc * yc, axis=-1, keepdims=True) + EPS)
        yn = yn * lg_ref[...] + lb_ref[...]
        h_scr[...] = (yn * jax.nn.sigmoid(yn)).astype(BF16)

    mix = jnp.dot(h_scr[...], w_ref[...], preferred_element_type=F32) + b2_ref[...]
    o_ref[...] = x_ref[...] + gt_ref[...] * mix


def _ln_out_call(y2, b_dw, ln_g, ln_b, w_pw2, b_pw2, x2, gt, rows_per_mod):
    m, d = x2.shape
    tm = _tile(rows_per_mod, 1024)
    per = rows_per_mod // tm
    tn = _tile(d, 1024)
    vec = pl.BlockSpec((1, d), lambda i, j: (0, 0))
    return pl.pallas_call(
        _ln_out_kernel,
        grid=(m // tm, d // tn),
        in_specs=[pl.BlockSpec((tm, d), lambda i, j: (i, 0)), vec, vec, vec,
                  pl.BlockSpec((d, tn), lambda i, j: (0, j)),
                  pl.BlockSpec((1, tn), lambda i, j: (0, j)),
                  pl.BlockSpec((tm, tn), lambda i, j: (i, j)),
                  pl.BlockSpec((None, 1, tn), lambda i, j: (i // per, 0, j))],
        out_specs=pl.BlockSpec((tm, tn), lambda i, j: (i, j)),
        out_shape=jax.ShapeDtypeStruct((m, d), F32),
        scratch_shapes=[pltpu.VMEM((tm, d), BF16)],
        compiler_params=_params(("parallel", "arbitrary")),
        name="conv_out",
    )(y2, b_dw, ln_g, ln_b, w_pw2, b_pw2, x2, gt)


def _vmax(a, b):
    if a is None:
        return b
    if b is None:
        return a
    return jnp.maximum(a, b)


def _vmin(a, b):
    if a is None or b is None:
        return None
    return jnp.minimum(a, b)


def _bitonic_merge_desc(xs):
    n = len(xs)
    j = n // 2
    while j >= 1:
        for i in range(n):
            o = i ^ j
            if o > i:
                a, b = xs[i], xs[o]
                xs[i], xs[o] = _vmax(a, b), _vmin(a, b)
        j //= 2
    return xs


def _bitonic_sort_desc(xs):
    n = len(xs)
    k = 2
    while k <= n:
        j = k // 2
        while j >= 1:
            for i in range(n):
                o = i ^ j
                if o > i:
                    a, b = xs[i], xs[o]
                    if (i & k) == 0:
                        xs[i], xs[o] = _vmax(a, b), _vmin(a, b)
                    else:
                        xs[i], xs[o] = _vmin(a, b), _vmax(a, b)
            j //= 2
        k *= 2
    return xs


def _merge_top(a, b):
    n = len(a)
    b = list(b) + [None] * (n - len(b))
    return _bitonic_merge_desc([_vmax(a[i], b[n - 1 - i]) for i in range(n)])


def _topk_kernel(q_ref, keys_ref, th_ref, e1_ref, s2_ref, e2_ref, s_scr):
    heads = th_ref.shape[0]
    nk = keys_ref.shape[1]
    k = PEER_TOPK
    t = q_ref.shape[0]
    sub = lax.broadcasted_iota(jnp.int32, (SUBLANES, t), 0)
    tops = [[None] * k, [None] * k]
    for h in range(heads):
        for p in range(2):
            hp = 2 * h + p
            dk = keys_ref.shape[2]
            s = lax.dot_general(keys_ref[hp], q_ref[:, hp * dk:(hp + 1) * dk],
                                (((1,), (1,)), ((), ())), preferred_element_type=F32)
            s_scr[hp] = s
            groups = _bitonic_sort_desc([s[SUBLANES * a:SUBLANES * (a + 1), :] for a in range(nk // SUBLANES)])
            assert len(groups) == k
            for shift in (4, 2, 1):
                groups = _merge_top(groups, [pltpu.roll(g, shift, 0) for g in groups])
            for a in range(k):
                prev = tops[p][a]
                tops[p][a] = groups[a] if prev is None else jnp.where(sub == h, groups[a], prev)
    v1, v2 = tops
    best = [v1[0] + v2[b] for b in range(k)]
    for a in range(1, k // 2):
        best = _merge_top(best, [v1[a] + v2[b] for b in range(k // (a + 1))])
    best = _merge_top(best, [v1[a] + v2[0] for a in range(k // 2, k)])
    cmax = best[0]
    zsum = jnp.ones_like(cmax)
    for a in range(1, k):
        zsum = zsum + jnp.exp(best[a] - cmax)
    tau = best[k - 1]
    inv_z = 1.0 / zsum
    for h in range(heads):
        s1 = s_scr[2 * h]
        s2 = s_scr[2 * h + 1]
        tau_h = tau[h:h + 1, :]
        slack = (jnp.abs(tau_h) + jnp.abs(s1)) * (2.0 ** -22)
        th_ref[h] = (tau_h - s1) - slack
        e1_ref[h] = jnp.exp(s1 - v1[0][h:h + 1, :])
        s2_ref[h] = s2
        e2_ref[h] = jnp.exp(s2 - v2[0][h:h + 1, :]) * inv_z[h:h + 1, :]


def _topk_call(qk, keys):
    m = qk.shape[0]
    hp, nk, dk = keys.shape
    heads = hp // 2
    t = _tile(m, 512)
    out = jax.ShapeDtypeStruct((heads, nk, m), F32)
    ospec = pl.BlockSpec((heads, nk, t), lambda i: (0, 0, i))
    return pl.pallas_call(
        _topk_kernel,
        grid=(m // t,),
        in_specs=[pl.BlockSpec((t, hp * dk), lambda i: (i, 0)),
                  pl.BlockSpec((hp, nk, dk), lambda i: (0, 0, 0))],
        out_specs=[ospec, ospec, ospec, ospec],
        out_shape=[out, out, out, out],
        scratch_shapes=[pltpu.VMEM((hp, nk, t), F32)],
        compiler_params=_params(("parallel",)),
        name="peer_topk",
    )(qk, keys)


def _peer_weights(i, a, g_out, row0, bc_scr, th_ref, e1_ref, s2_ref, e2_ref):
    heads, nk, t = th_ref.shape
    for h in range(heads):
        bc_scr[0, h] = jnp.broadcast_to(th_ref[h, pl.ds(i, 1), :], (SUBLANES, t))
        bc_scr[1, h] = jnp.broadcast_to(e1_ref[h, pl.ds(i, 1), :], (SUBLANES, t))
    for tl in range(t // LANES):
        lanes = slice(tl * LANES, (tl + 1) * LANES)
        w = None
        for h in range(heads):
            th = jnp.concatenate([bc_scr[0, h, :, lanes]] * (nk // SUBLANES), axis=0)
            e1 = jnp.concatenate([bc_scr[1, h, :, lanes]] * (nk // SUBLANES), axis=0)
            term = jnp.where(s2_ref[h, :, lanes] >= th, e2_ref[h, :, lanes], 0.0) * e1
            w = term if w is None else w + term
        a_t = a[:, lanes]
        act = 0.5 * a_t * (1.0 + lax.erf(a_t * (1.0 / math.sqrt(2.0))))
        g_out[row0:row0 + nk, lanes] = (w * act).astype(BF16)


def _peer_kernel(x_ref, g_ref, sh_ref, sc_ref, gt_ref, gfin_ref, th_ref, e1_ref, s2_ref, e2_ref,
                 u_ref, vtp_ref, vtc_ref, vtl_ref, o_ref, ht_scr, ga_scr, gb_scr, acc_scr, bc_scr,
                 *, final_norm):
    kb = pl.program_id(1)
    nk = th_ref.shape[1]
    n_sub = PEER_EB // PEER_SUB
    sel = (th_ref, e1_ref, s2_ref, e2_ref)

    @pl.when(kb == 0)
    def _():
        h = _rmsnorm_mod(x_ref[...], g_ref[...], sh_ref[...], sc_ref[...])
        ht_scr[...] = h.T.astype(BF16)
        gb_scr[...] = jnp.zeros(gb_scr.shape, BF16)
        acc_scr[...] = jnp.zeros(acc_scr.shape, F32)

    def first_matmul(idx):
        rows = slice(idx * PEER_SUB, (idx + 1) * PEER_SUB)
        return jnp.dot(u_ref[rows, :], ht_scr[...], preferred_element_type=F32)

    a_next = first_matmul(0)
    for idx in range(2 * n_sub):
        phase, sb = divmod(idx, n_sub)
        a = a_next
        if idx + 1 < 2 * n_sub:
            a_next = first_matmul(idx + 1)
        if sb == min(1, n_sub - 1):
            vt_ref, g_in = (vtp_ref, gb_scr) if phase == 0 else (vtc_ref, ga_scr)
            acc_scr[...] += jnp.dot(vt_ref[...], g_in[...], preferred_element_type=F32)
        g_out = ga_scr if phase == 0 else gb_scr
        for gi in range(PEER_SUB // nk):
            i = kb * (2 * PEER_EB // nk) + idx * (PEER_SUB // nk) + gi
            _peer_weights(i, a[gi * nk:(gi + 1) * nk, :], g_out, sb * PEER_SUB + gi * nk, bc_scr, *sel)

    @pl.when(kb == pl.num_programs(1) - 1)
    def _():
        acc = acc_scr[...] + jnp.dot(vtl_ref[...], gb_scr[...], preferred_element_type=F32)
        y = x_ref[...] + gt_ref[...] * acc.T
        if final_norm:
            y = y * lax.rsqrt(jnp.mean(y * y, axis=-1, keepdims=True) + EPS) * gfin_ref[...]
        o_ref[...] = y


def _peer_call(x2, g, sh, sc, gt, g_final, th, e1, s2, e2, u, vt, rows_per_mod, final_norm):
    m, d = x2.shape
    heads, nk, _ = th.shape
    e = u.shape[0]
    t = _tile(rows_per_mod, 512)
    per = rows_per_mod // t
    eb = PEER_EB
    nkb = e // (2 * eb)
    assert nkb * 2 * eb == e
    sel = pl.BlockSpec((heads, nk, t), lambda i, k: (0, 0, i))
    mod = pl.BlockSpec((None, 1, d), lambda i, k: (i // per, 0, 0))
    vec = pl.BlockSpec((1, d), lambda i, k: (0, 0))
    return pl.pallas_call(
        functools.partial(_peer_kernel, final_norm=final_norm),
        grid=(m // t, nkb),
        in_specs=[pl.BlockSpec((t, d), lambda i, k: (i, 0)),
                  vec, mod, mod, mod, vec, sel, sel, sel, sel,
                  pl.BlockSpec((2 * eb, d), lambda i, k: (k, 0)),
                  pl.BlockSpec((d, eb), lambda i, k: (0, jnp.maximum(2 * k - 1, 0))),
                  pl.BlockSpec((d, eb), lambda i, k: (0, 2 * k)),
                  pl.BlockSpec((d, eb), lambda i, k: (0, 2 * nkb - 1))],
        out_specs=pl.BlockSpec((t, d), lambda i, k: (i, 0)),
        out_shape=jax.ShapeDtypeStruct((m, d), F32),
        scratch_shapes=[pltpu.VMEM((d, t), BF16), pltpu.VMEM((eb, t), BF16),
                        pltpu.VMEM((eb, t), BF16), pltpu.VMEM((d, t), F32),
                        pltpu.VMEM((2, heads, SUBLANES, t), F32)],
        compiler_params=_params(("parallel", "arbitrary")),
        name="peer_experts",
    )(x2, g, sh, sc, gt, g_final, th, e1, s2, e2, u, vt, vt, vt)


def _peer_layer(x2, g_ffn, sh, sc, gt, g_final, w_q, keys, u, v, seq, final_norm):
    qk = _nmm_call(x2, g_ffn, sh, sc, w_q.astype(BF16), seq, BF16)
    kb = keys.reshape(2 * PEER_HEADS, PEER_NKEYS, PEER_DKEY // 2).astype(BF16)
    th, e1, s2, e2 = _topk_call(qk, kb)
    return _peer_call(x2, g_ffn, sh, sc, gt, g_final, th, e1, s2, e2,
                      u.astype(BF16), v.astype(BF16).T, seq, final_norm)


def kernel(x, c, ctx, c_ctx, w_ada, b_ada, g_mix, g_ffn, hg_w_in, hg_w_out, hg_gnorm, hg_lb,
           cv_w_pw1, cv_b_pw1, cv_w_dw, cv_b_dw, cv_ln_g, cv_ln_b, cv_w_pw2, cv_b_pw2,
           peer_w_q, peer_keys, peer_u, peer_v, g_final):
    b, seq, d = x.shape
    ctx_len = ctx.shape[1]
    depth = w_ada.shape[0]
    heads = d // LANES
    assert heads * LANES == d and seq % SCAN_CHUNK == 0 and ctx_len % SCAN_CHUNK == 0

    pad_rows = (-(b + 1)) % SUBLANES
    cc = jnp.concatenate([c, c_ctx[None, :], jnp.zeros((pad_rows, d), F32)], axis=0)
    ada = _ada_call(cc, w_ada, b_ada)

    x2 = x.reshape(b * seq, d)
    for l in range(depth):
        mixer = l % N_MIXERS
        mods = ada[l, :b].reshape(b, 1, 6, d)
        sh_m, sc_m, gt_m, sh_f, sc_f, gt_f = [mods[:, :, i, :] for i in range(6)]
        g_mix_l = g_mix[l][None, :]
        g_ffn_l = g_ffn[l][None, :]
        if mixer == 0:
            a = l // N_MIXERS
            w_in = hg_w_in[a].astype(BF16)
            cmods = ada[l, b].reshape(6, d)
            csh = jnp.broadcast_to(cmods[0][None, None, :], (b, 1, d))
            csc = jnp.broadcast_to(cmods[1][None, None, :], (b, 1, d))
            zc = _nmm_call(ctx.reshape(b * ctx_len, d), g_mix_l, csh, csc, w_in, ctx_len, F32)
            s0 = jnp.zeros((b, heads // 2, LANES, 2 * LANES), F32)
            _, _, s_f, s_b = _scan_call(zc.reshape(b, ctx_len, 5 * d), hg_lb, s0, s0, l)
            z = _nmm_call(x2, g_mix_l, sh_m, sc_m, w_in, seq, F32)
            o_f, o_b, _, _ = _scan_call(z.reshape(b, seq, 5 * d), hg_lb, s_f, s_b, l)
            x2 = _mix_out_call(o_f.reshape(b * seq, d), o_b.reshape(b * seq, d), z,
                               hg_gnorm[a][None, :], hg_w_out[a].astype(BF16), x2, gt_m, seq)
        else:
            bi = l // N_MIXERS
            y = _nmm_call(x2, g_mix_l, sh_m, sc_m, cv_w_pw1[bi].astype(BF16), seq, F32,
                          glu_bias=cv_b_pw1[bi][None, :])
            y = _conv_call(y.reshape(b, seq, d), cv_w_dw[bi])
            x2 = _ln_out_call(y.reshape(b * seq, d), cv_b_dw[bi][None, :], cv_ln_g[bi][None, :],
                              cv_ln_b[bi][None, :], cv_w_pw2[bi].astype(BF16),
                              cv_b_pw2[bi][None, :], x2, gt_m, seq)
        x2 = _peer_layer(x2, g_ffn_l, sh_f, sc_f, gt_f, g_final[None, :], peer_w_q[l], peer_keys[l],
                         peer_u[l], peer_v[l], seq, final_norm=(l == depth - 1))
    return x2.reshape(b, seq, d)
```

```python
import functools
import math

import numpy as np
import jax
import jax.numpy as jnp
from jax import lax
from jax.experimental import pallas as pl
from jax.experimental.pallas import tpu as pltpu

EPS = 1e-6
GRID_W = 64
CONV_WIDTH = 31
HG_HEADS = 8
PEER_HEADS = 8
PEER_NKEYS = 128
PEER_DKEY = 256
PEER_TOPK = 16
N_MIXERS = 2

LANES = 128
SUBLANES = 8
SCAN_CHUNK = 128
SCAN_MATMUL_LEVELS = int(math.log2(SUBLANES))
PEER_EB = 512
PEER_SUB = 256
MXU_WIDTH = 256
VMEM_LIMIT = 56 * 1024 * 1024

F32 = jnp.float32
BF16 = jnp.bfloat16


def _tile(n, pref):
    t = min(n, pref)
    while n % t:
        t //= 2
    return t


def _params(sem):
    return pltpu.CompilerParams(dimension_semantics=sem, vmem_limit_bytes=VMEM_LIMIT)


def _rmsnorm_mod(x, g, sh, sc):
    y = x * lax.rsqrt(jnp.mean(x * x, axis=-1, keepdims=True) + EPS) * g
    return y * (1.0 + sc) + sh


def _ada_kernel(c_ref, w_ref, b_ref, o_ref):
    s = c_ref[...]
    s = s * jax.nn.sigmoid(s)
    o_ref[...] = jnp.dot(s, w_ref[...], preferred_element_type=F32,
                         precision=lax.Precision.HIGHEST) + b_ref[...]


def _ada_call(cc, w_ada, b_ada):
    depth, d, n = w_ada.shape
    r = cc.shape[0]
    tn = _tile(n, 1024)
    return pl.pallas_call(
        _ada_kernel,
        grid=(depth, n // tn),
        in_specs=[pl.BlockSpec((r, d), lambda l, j: (0, 0)),
                  pl.BlockSpec((None, d, tn), lambda l, j: (l, 0, j)),
                  pl.BlockSpec((None, 1, tn), lambda l, j: (l, 0, j))],
        out_specs=pl.BlockSpec((None, r, tn), lambda l, j: (l, 0, j)),
        out_shape=jax.ShapeDtypeStruct((depth, r, n), F32),
        compiler_params=_params(("parallel", "parallel")),
        name="ada",
    )(cc, w_ada, b_ada.reshape(depth, 1, n))


def _nmm_kernel(x_ref, g_ref, sh_ref, sc_ref, w_ref, o_ref, h_scr):
    @pl.when(pl.program_id(1) == 0)
    def _():
        h_scr[...] = _rmsnorm_mod(x_ref[...], g_ref[...], sh_ref[...], sc_ref[...]).astype(BF16)

    o_ref[...] = jnp.dot(h_scr[...], w_ref[...], preferred_element_type=F32).astype(o_ref.dtype)


def _nmm_glu_kernel(x_ref, g_ref, sh_ref, sc_ref, wa_ref, wg_ref, ba_ref, bg_ref, o_ref, h_scr):
    @pl.when(pl.program_id(1) == 0)
    def _():
        h_scr[...] = _rmsnorm_mod(x_ref[...], g_ref[...], sh_ref[...], sc_ref[...]).astype(BF16)

    h = h_scr[...]
    a = jnp.dot(h, wa_ref[...], preferred_element_type=F32) + ba_ref[...]
    g = jnp.dot(h, wg_ref[...], preferred_element_type=F32) + bg_ref[...]
    o_ref[...] = (a * jax.nn.sigmoid(g)).astype(o_ref.dtype)


def _nmm_call(x2, g, sh, sc, w, rows_per_mod, out_dtype, glu_bias=None):
    m, d = x2.shape
    n = w.shape[1]
    tm = _tile(rows_per_mod, 1024)
    per = rows_per_mod // tm
    x_spec = pl.BlockSpec((tm, d), lambda i, j: (i, 0))
    g_spec = pl.BlockSpec((1, d), lambda i, j: (0, 0))
    mod_spec = pl.BlockSpec((None, 1, d), lambda i, j: (i // per, 0, 0))
    scratch = [pltpu.VMEM((tm, d), BF16)]
    if glu_bias is None:
        tn = _tile(n, 1024)
        return pl.pallas_call(
            _nmm_kernel,
            grid=(m // tm, n // tn),
            in_specs=[x_spec, g_spec, mod_spec, mod_spec,
                      pl.BlockSpec((d, tn), lambda i, j: (0, j))],
            out_specs=pl.BlockSpec((tm, tn), lambda i, j: (i, j)),
            out_shape=jax.ShapeDtypeStruct((m, n), out_dtype),
            scratch_shapes=scratch,
            compiler_params=_params(("parallel", "arbitrary")),
            name="norm_mod_matmul",
        )(x2, g, sh, sc, w)
    nh = n // 2
    tn = _tile(nh, 1024)
    nb = nh // tn
    return pl.pallas_call(
        _nmm_glu_kernel,
        grid=(m // tm, nb),
        in_specs=[x_spec, g_spec, mod_spec, mod_spec,
                  pl.BlockSpec((d, tn), lambda i, j: (0, j)),
                  pl.BlockSpec((d, tn), lambda i, j: (0, j + nb)),
                  pl.BlockSpec((1, tn), lambda i, j: (0, j)),
                  pl.BlockSpec((1, tn), lambda i, j: (0, j + nb))],
        out_specs=pl.BlockSpec((tm, tn), lambda i, j: (i, j)),
        out_shape=jax.ShapeDtypeStruct((m, nh), out_dtype),
        scratch_shapes=scratch,
        compiler_params=_params(("parallel", "arbitrary")),
        name="norm_mod_matmul_glu",
    )(x2, g, sh, sc, w, w, glu_bias, glu_bias)


@functools.lru_cache(maxsize=None)
def _scan_consts(c):
    nl = int(math.log2(c))
    assert 1 << nl == c
    idx = np.arange(c)
    mats = np.zeros((2, 2 + SCAN_MATMUL_LEVELS, c, c), np.float32)
    level = np.zeros((2, c, c), np.int32)
    for d in range(2):
        tau = idx if d == 0 else c - 1 - idx
        tt, tr = tau[:, None], tau[None, :]
        mats[d, 0] = tr <= tt
        mats[d, 1] = tr > tt
        for l in range(SCAN_MATMUL_LEVELS):
            half = 1 << l
            mid = (tau // (2 * half)) * (2 * half) + half - 1
            later = ((tau >> l) & 1) == 1
            mats[d, 2 + l] = ((later[:, None] & (tr > mid[:, None]) & (tr <= tt))
                              | ((~later)[:, None] & (tr > tt) & (tr <= mid[:, None])))
        xor = tt ^ tr
        lev = np.floor(np.log2(np.maximum(xor, 1))).astype(np.int32)
        level[d] = np.where(tt > tr, lev, np.where(tt == tr, -1, -2))
    return mats.reshape(2, (2 + SCAN_MATMUL_LEVELS) * c, c), level, nl


def _scan_direction(d, zf, v, q, lb, est_ref, lev_ref, st_ref, o_ref, n_levels):
    c = zf.shape[0]
    f = lb + (1.0 - lb) * jax.nn.sigmoid(zf)
    logf = jnp.log(f)
    kk = (1.0 - lb) * jax.nn.sigmoid(-zf)
    lev = lev_ref[d]

    def block_diag(x):
        zero = jnp.zeros((x.shape[0], LANES), BF16)
        return jnp.concatenate([jnp.concatenate([x[:, :LANES], zero], axis=1),
                                jnp.concatenate([zero, x[:, LANES:]], axis=1)], axis=0)

    nt = (((1,), (1,)), ((), ()))
    for hp in range(st_ref.shape[0]):
        sl = slice(2 * hp * LANES, 2 * (hp + 1) * LANES)
        lf = logf[:, sl]
        hi = lf.astype(BF16)
        mid = (lf - hi.astype(F32)).astype(BF16)
        ex = jnp.dot(est_ref[d], jnp.concatenate([hi, mid], axis=0), preferred_element_type=F32)
        p = jnp.exp(ex)
        dec = jnp.exp(ex[0:1, :] + ex[c:c + 1, :])
        q_h, k_h, v_h = q[:, sl], kk[:, sl], v[:, sl]
        st = st_ref[hp]
        qb = (q_h * p[0:c]).astype(BF16)
        inter = lax.dot_general(qb, block_diag(st.astype(BF16)), nt, preferred_element_type=F32)
        pr = lax.dot_general(q_h.astype(BF16), block_diag(k_h.astype(BF16)), nt,
                             preferred_element_type=F32)
        scores = jnp.where(lev == -1, pr, 0.0)
        b = ex[0:c]
        for l in range(n_levels):
            if l < SCAN_MATMUL_LEVELS:
                pl_ = p[(2 + l) * c:(3 + l) * c]
            else:
                half = 1 << l
                rows = [jnp.broadcast_to(b[p0 + half - 1 + d:p0 + half + d, :], (2 * half, 2 * LANES))
                        for p0 in range(0, c, 2 * half)]
                bmid = jnp.concatenate(rows, axis=0) if len(rows) > 1 else rows[0]
                pl_ = jnp.exp(-jnp.abs(b - bmid))
            ql = (q_h * pl_).astype(BF16)
            kl = (k_h * pl_).astype(BF16)
            pr = lax.dot_general(ql, block_diag(kl), nt, preferred_element_type=F32)
            scores = jnp.where(lev == l, pr, scores)
        intra = jnp.dot(scores.astype(BF16), block_diag(v_h.astype(BF16)), preferred_element_type=F32)
        o_ref[:, sl] = inter + intra
        kend = (k_h * p[c:2 * c]).astype(BF16)
        vt = jnp.concatenate([v_h[:, :LANES].T, v_h[:, LANES:].T], axis=1).astype(BF16)
        st_ref[hp] = st * dec + jnp.dot(vt, block_diag(kend), preferred_element_type=F32)


def _scan_kernel(lbraw_ref, est_ref, lev_ref, s0f_ref, s0b_ref,
                 vf_ref, ff_ref, qf_ref, vb_ref, fb_ref, qb_ref,
                 of_ref, ob_ref, sf_ref, sb_ref, stf_scr, stb_scr, *, layer, n_levels):
    j = pl.program_id(1)

    @pl.when(j == 0)
    def _():
        stf_scr[...] = s0f_ref[...]
        stb_scr[...] = s0b_ref[...]

    raw = lbraw_ref[...]
    e = jnp.exp(raw - jnp.max(raw, axis=0, keepdims=True))
    lb_all = jnp.sum(e[0:layer + 1], axis=0, keepdims=True) / jnp.sum(e, axis=0, keepdims=True)
    d_model = vf_ref.shape[-1]
    _scan_direction(0, ff_ref[...], vf_ref[...], qf_ref[...], lb_all[:, :d_model],
                    est_ref, lev_ref, stf_scr, of_ref, n_levels)
    _scan_direction(1, fb_ref[...], vb_ref[...], qb_ref[...], lb_all[:, d_model:],
                    est_ref, lev_ref, stb_scr, ob_ref, n_levels)

    @pl.when(j == pl.num_programs(1) - 1)
    def _():
        sf_ref[...] = stf_scr[...]
        sb_ref[...] = stb_scr[...]


def _state_direction(d, zf, v, lb, est_ref, st_ref):
    c = zf.shape[0]
    f = lb + (1.0 - lb) * jax.nn.sigmoid(zf)
    logf = jnp.log(f)
    kk = (1.0 - lb) * jax.nn.sigmoid(-zf)
    for hp in range(st_ref.shape[0]):
        sl = slice(2 * hp * LANES, 2 * (hp + 1) * LANES)
        lf = logf[:, sl]
        hi = lf.astype(BF16)
        mid = (lf - hi.astype(F32)).astype(BF16)
        ex = jnp.dot(est_ref[d, 0:2 * c, :], jnp.concatenate([hi, mid], axis=0),
                     preferred_element_type=F32)
        dec = jnp.exp(ex[0:1, :] + ex[c:c + 1, :])
        k_h, v_h = kk[:, sl], v[:, sl]
        kend = (k_h * jnp.exp(ex[c:2 * c])).astype(BF16)
        zero = jnp.zeros((c, LANES), BF16)
        kend_bd = jnp.concatenate([jnp.concatenate([kend[:, :LANES], zero], axis=1),
                                   jnp.concatenate([zero, kend[:, LANES:]], axis=1)], axis=0)
        vt = jnp.concatenate([v_h[:, :LANES].T, v_h[:, LANES:].T], axis=1).astype(BF16)
        st_ref[hp] = st_ref[hp] * dec + jnp.dot(vt, kend_bd, preferred_element_type=F32)


def _state_kernel(lbraw_ref, est_ref, s0f_ref, s0b_ref, vf_ref, ff_ref, vb_ref, fb_ref,
                  sf_ref, sb_ref, stf_scr, stb_scr, *, layer):
    j = pl.program_id(1)

    @pl.when(j == 0)
    def _():
        stf_scr[...] = s0f_ref[...]
        stb_scr[...] = s0b_ref[...]

    raw = lbraw_ref[...]
    e = jnp.exp(raw - jnp.max(raw, axis=0, keepdims=True))
    lb_all = jnp.sum(e[0:layer + 1], axis=0, keepdims=True) / jnp.sum(e, axis=0, keepdims=True)
    d_model = vf_ref.shape[-1]
    _state_direction(0, ff_ref[...], vf_ref[...], lb_all[:, :d_model], est_ref, stf_scr)
    _state_direction(1, fb_ref[...], vb_ref[...], lb_all[:, d_model:], est_ref, stb_scr)

    @pl.when(j == pl.num_programs(1) - 1)
    def _():
        sf_ref[...] = stf_scr[...]
        sb_ref[...] = stb_scr[...]


def _state_call(z, hg_lb, s0f, s0b, layer):
    b, l, d3 = z.shape
    d = d3 // 3
    pairs = d // (2 * LANES)
    c = SCAN_CHUNK
    nc = l // c
    est, _, _ = _scan_consts(c)
    est = jnp.asarray(np.concatenate([est, est], axis=2), BF16)

    def zspec(col, rev):
        if rev:
            return pl.BlockSpec((None, c, d), lambda bi, j: (bi, nc - 1 - j, col))
        return pl.BlockSpec((None, c, d), lambda bi, j: (bi, j, col))

    full = lambda a: pl.BlockSpec(a.shape, lambda bi, j: (0,) * a.ndim)
    st_spec = pl.BlockSpec((None, pairs, LANES, 2 * LANES), lambda bi, j: (bi, 0, 0, 0))
    return pl.pallas_call(
        functools.partial(_state_kernel, layer=layer),
        grid=(b, nc),
        in_specs=[full(hg_lb), full(est), st_spec, st_spec,
                  zspec(0, False), zspec(1, False), zspec(0, True), zspec(2, True)],
        out_specs=[st_spec, st_spec],
        out_shape=[jax.ShapeDtypeStruct(s0f.shape, F32), jax.ShapeDtypeStruct(s0b.shape, F32)],
        scratch_shapes=[pltpu.VMEM((pairs, LANES, 2 * LANES), F32),
                        pltpu.VMEM((pairs, LANES, 2 * LANES), F32)],
        compiler_params=_params(("parallel", "arbitrary")),
        name="hgrn2_state",
    )(hg_lb, est, s0f, s0b, z, z, z, z)


def _scan_call(z, hg_lb, s0f, s0b, layer):
    b, l, d5 = z.shape
    d = d5 // 5
    pairs = d // (2 * LANES)
    c = SCAN_CHUNK
    nc = l // c
    est, lev, n_levels = _scan_consts(c)
    est = jnp.asarray(np.concatenate([est, est], axis=2), BF16)
    lev = jnp.asarray(np.concatenate([lev, lev], axis=2))

    def zspec(col, rev):
        if rev:
            return pl.BlockSpec((None, c, d), lambda bi, j: (bi, nc - 1 - j, col))
        return pl.BlockSpec((None, c, d), lambda bi, j: (bi, j, col))

    full = lambda a: pl.BlockSpec(a.shape, lambda bi, j: (0,) * a.ndim)
    st_spec = pl.BlockSpec((None, pairs, LANES, 2 * LANES), lambda bi, j: (bi, 0, 0, 0))
    return pl.pallas_call(
        functools.partial(_scan_kernel, layer=layer, n_levels=n_levels),
        grid=(b, nc),
        in_specs=[full(hg_lb), full(est), full(lev), st_spec, st_spec,
                  zspec(0, False), zspec(1, False), zspec(3, False),
                  zspec(0, True), zspec(2, True), zspec(3, True)],
        out_specs=[pl.BlockSpec((None, c, d), lambda bi, j: (bi, j, 0)),
                   pl.BlockSpec((None, c, d), lambda bi, j: (bi, nc - 1 - j, 0)),
                   st_spec, st_spec],
        out_shape=[jax.ShapeDtypeStruct((b, l, d), F32), jax.ShapeDtypeStruct((b, l, d), F32),
                   jax.ShapeDtypeStruct(s0f.shape, F32), jax.ShapeDtypeStruct(s0b.shape, F32)],
        scratch_shapes=[pltpu.VMEM((pairs, LANES, 2 * LANES), F32),
                        pltpu.VMEM((pairs, LANES, 2 * LANES), F32)],
        compiler_params=_params(("parallel", "arbitrary")),
        name="hgrn2_scan",
    )(hg_lb, est, lev, s0f, s0b, z, z, z, z, z, z)


def _mix_out_kernel(of_ref, ob_ref, gate_ref, gn_ref, w_ref, x_ref, gt_ref, o_ref, h_scr):
    @pl.when(pl.program_id(1) == 0)
    def _():
        o = of_ref[...] + ob_ref[...]
        gate = gate_ref[...]
        gn = gn_ref[...]
        for h in range(o.shape[1] // LANES):
            sl = slice(h * LANES, (h + 1) * LANES)
            oh = o[:, sl]
            y = oh * lax.rsqrt(jnp.mean(oh * oh, axis=-1, keepdims=True) + EPS) * gn
            gh = gate[:, sl]
            h_scr[:, sl] = (y * (gh * jax.nn.sigmoid(gh))).astype(BF16)

    mix = jnp.dot(h_scr[...], w_ref[...], preferred_element_type=F32)
    o_ref[...] = x_ref[...] + gt_ref[...] * mix


def _mix_out_call(o_f, o_b, z2, gnorm, w_out, x2, gt, rows_per_mod):
    m, d = x2.shape
    tm = _tile(rows_per_mod, 1024)
    per = rows_per_mod // tm
    tn = _tile(d, 1024)
    row = pl.BlockSpec((tm, d), lambda i, j: (i, 0))
    return pl.pallas_call(
        _mix_out_kernel,
        grid=(m // tm, d // tn),
        in_specs=[row, row,
                  pl.BlockSpec((tm, d), lambda i, j: (i, 4)),
                  pl.BlockSpec((1, LANES), lambda i, j: (0, 0)),
                  pl.BlockSpec((d, tn), lambda i, j: (0, j)),
                  pl.BlockSpec((tm, tn), lambda i, j: (i, j)),
                  pl.BlockSpec((None, 1, tn), lambda i, j: (i // per, 0, j))],
        out_specs=pl.BlockSpec((tm, tn), lambda i, j: (i, j)),
        out_shape=jax.ShapeDtypeStruct((m, d), F32),
        scratch_shapes=[pltpu.VMEM((tm, d), BF16)],
        compiler_params=_params(("parallel", "arbitrary")),
        name="hgrn2_out",
    )(o_f, o_b, z2, gnorm, w_out, x2, gt)


def _conv_kernel(y_ref, w_ref, o_ref, padw_scr, padh_scr, *, rows, n_w_tiles):
    ct = pl.program_id(1)
    half = (CONV_WIDTH - 1) // 2
    l, lanes = y_ref.shape
    lead = padw_scr.shape[1] - GRID_W - 16

    @pl.when(ct < n_w_tiles)
    def _():
        padw_scr[...] = jnp.zeros(padw_scr.shape, F32)
        padw_scr[:, lead:lead + GRID_W, :] = y_ref[...].reshape(rows, GRID_W, lanes)
        acc = jnp.zeros((rows, GRID_W, lanes), F32)
        for k in range(CONV_WIDTH):
            start = lead + k - half
            acc = acc + w_ref[k:k + 1, :].reshape(1, 1, lanes) * padw_scr[:, start:start + GRID_W, :]
        o_ref[...] = acc.reshape(l, lanes)

    @pl.when(ct >= n_w_tiles)
    def _():
        margin = half * GRID_W
        padh_scr[...] = jnp.zeros(padh_scr.shape, F32)
        padh_scr[margin:margin + l, :] = y_ref[...]
        acc = jnp.zeros((l, lanes), F32)
        for k in range(CONV_WIDTH):
            acc = acc + w_ref[k:k + 1, :] * padh_scr[k * GRID_W:k * GRID_W + l, :]
        o_ref[...] = acc


def _conv_call(y, w_dw):
    b, l, c = y.shape
    rows = l // GRID_W
    half = (CONV_WIDTH - 1) // 2
    n_w_tiles = (c // 2) // LANES
    wp = jnp.pad(w_dw, ((0, 32 - CONV_WIDTH), (0, 0)))
    return pl.pallas_call(
        functools.partial(_conv_kernel, rows=rows, n_w_tiles=n_w_tiles),
        grid=(b, c // LANES),
        in_specs=[pl.BlockSpec((None, l, LANES), lambda bi, ct: (bi, 0, ct)),
                  pl.BlockSpec((32, LANES), lambda bi, ct: (0, ct))],
        out_specs=pl.BlockSpec((None, l, LANES), lambda bi, ct: (bi, 0, ct)),
        out_shape=jax.ShapeDtypeStruct((b, l, c), F32),
        scratch_shapes=[pltpu.VMEM((rows, GRID_W + 32, LANES), F32),
                        pltpu.VMEM((l + 2 * half * GRID_W, LANES), F32)],
        compiler_params=_params(("parallel", "parallel")),
        name="axial_dwconv",
    )(y, wp)


def _ln_out_kernel(y_ref, bdw_ref, lg_ref, lb_ref, w_ref, b2_ref, x_ref, gt_ref, o_ref, h_scr):
    @pl.when(pl.program_id(1) == 0)
    def _():
        y = y_ref[...] + bdw_ref[...]
        yc = y - jnp.mean(y, axis=-1, keepdims=True)
        yn = yc * lax.rsqrt(jnp.mean(yc * yc, axis=-1, keepdims=True) + EPS)
        yn = yn * lg_ref[...] + lb_ref[...]
        h_scr[...] = (yn * jax.nn.sigmoid(yn)).astype(BF16)

    mix = jnp.dot(h_scr[...], w_ref[...], preferred_element_type=F32) + b2_ref[...]
    o_ref[...] = x_ref[...] + gt_ref[...] * mix


def _ln_out_call(y2, b_dw, ln_g, ln_b, w_pw2, b_pw2, x2, gt, rows_per_mod):
    m, d = x2.shape
    tm = _tile(rows_per_mod, 1024)
    per = rows_per_mod // tm
    tn = _tile(d, 1024)
    vec = pl.BlockSpec((1, d), lambda i, j: (0, 0))
    return pl.pallas_call(
        _ln_out_kernel,
        grid=(m // tm, d // tn),
        in_specs=[pl.BlockSpec((tm, d), lambda i, j: (i, 0)), vec, vec, vec,
                  pl.BlockSpec((d, tn), lambda i, j: (0, j)),
                  pl.BlockSpec((1, tn), lambda i, j: (0, j)),
                  pl.BlockSpec((tm, tn), lambda i, j: (i, j)),
                  pl.BlockSpec((None, 1, tn), lambda i, j: (i // per, 0, j))],
        out_specs=pl.BlockSpec((tm, tn), lambda i, j: (i, j)),
        out_shape=jax.ShapeDtypeStruct((m, d), F32),
        scratch_shapes=[pltpu.VMEM((tm, d), BF16)],
        compiler_params=_params(("parallel", "arbitrary")),
        name="conv_out",
    )(y2, b_dw, ln_g, ln_b, w_pw2, b_pw2, x2, gt)


def _vmax(a, b):
    if a is None:
        return b
    if b is None:
        return a
    return jnp.maximum(a, b)


def _vmin(a, b):
    if a is None or b is None:
        return None
    return jnp.minimum(a, b)


def _bitonic_merge_desc(xs):
    n = len(xs)
    j = n // 2
    while j >= 1:
        for i in range(n):
            o = i ^ j
            if o > i:
                a, b = xs[i], xs[o]
                xs[i], xs[o] = _vmax(a, b), _vmin(a, b)
        j //= 2
    return xs


def _bitonic_sort_desc(xs):
    n = len(xs)
    k = 2
    while k <= n:
        j = k // 2
        while j >= 1:
            for i in range(n):
                o = i ^ j
                if o > i:
                    a, b = xs[i], xs[o]
                    if (i & k) == 0:
                        xs[i], xs[o] = _vmax(a, b), _vmin(a, b)
                    else:
                        xs[i], xs[o] = _vmin(a, b), _vmax(a, b)
            j //= 2
        k *= 2
    return xs


def _merge_top(a, b):
    n = len(a)
    b = list(b) + [None] * (n - len(b))
    return _bitonic_merge_desc([_vmax(a[i], b[n - 1 - i]) for i in range(n)])


def _topk_kernel(q_ref, keys_ref, th_ref, e1_ref, s2_ref, e2_ref, s_scr):
    heads = th_ref.shape[0]
    nk = keys_ref.shape[1]
    k = PEER_TOPK
    t = q_ref.shape[0]
    sub = lax.broadcasted_iota(jnp.int32, (SUBLANES, t), 0)
    tops = [[None] * k, [None] * k]
    for h in range(heads):
        for p in range(2):
            hp = 2 * h + p
            dk = keys_ref.shape[2]
            s = lax.dot_general(keys_ref[hp], q_ref[:, hp * dk:(hp + 1) * dk],
                                (((1,), (1,)), ((), ())), preferred_element_type=F32)
            s_scr[hp] = s
            groups = _bitonic_sort_desc([s[SUBLANES * a:SUBLANES * (a + 1), :] for a in range(nk // SUBLANES)])
            assert len(groups) == k
            for shift in (4, 2, 1):
                groups = _merge_top(groups, [pltpu.roll(g, shift, 0) for g in groups])
            for a in range(k):
                prev = tops[p][a]
                tops[p][a] = groups[a] if prev is None else jnp.where(sub == h, groups[a], prev)
    v1, v2 = tops
    best = [v1[0] + v2[b] for b in range(k)]
    for a in range(1, k // 2):
        best = _merge_top(best, [v1[a] + v2[b] for b in range(k // (a + 1))])
    best = _merge_top(best, [v1[a] + v2[0] for a in range(k // 2, k)])
    cmax = best[0]
    zsum = jnp.ones_like(cmax)
    for a in range(1, k):
        zsum = zsum + jnp.exp(best[a] - cmax)
    tau = best[k - 1]
    inv_z = 1.0 / zsum
    for h in range(heads):
        s1 = s_scr[2 * h]
        s2 = s_scr[2 * h + 1]
        tau_h = tau[h:h + 1, :]
        slack = (jnp.abs(tau_h) + jnp.abs(s1)) * (2.0 ** -22)
        th_ref[h] = (tau_h - s1) - slack
        e1_ref[h] = jnp.exp(s1 - v1[0][h:h + 1, :])
        s2_ref[h] = s2
        e2_ref[h] = jnp.exp(s2 - v2[0][h:h + 1, :]) * inv_z[h:h + 1, :]


def _topk_call(qk, keys):
    m = qk.shape[0]
    hp, nk, dk = keys.shape
    heads = hp // 2
    t = _tile(m, 512)
    out = jax.ShapeDtypeStruct((heads, nk, m), F32)
    ospec = pl.BlockSpec((heads, nk, t), lambda i: (0, 0, i))
    return pl.pallas_call(
        _topk_kernel,
        grid=(m // t,),
        in_specs=[pl.BlockSpec((t, hp * dk), lambda i: (i, 0)),
                  pl.BlockSpec((hp, nk, dk), lambda i: (0, 0, 0))],
        out_specs=[ospec, ospec, ospec, ospec],
        out_shape=[out, out, out, out],
        scratch_shapes=[pltpu.VMEM((hp, nk, t), F32)],
        compiler_params=_params(("parallel",)),
        name="peer_topk",
    )(qk, keys)


def _peer_weights(i, a, g_out, row0, bc_scr, th_ref, e1_ref, s2_ref, e2_ref):
    heads, nk, t = th_ref.shape
    for h in range(heads):
        bc_scr[0, h] = jnp.broadcast_to(th_ref[h, pl.ds(i, 1), :], (SUBLANES, t))
        bc_scr[1, h] = jnp.broadcast_to(e1_ref[h, pl.ds(i, 1), :], (SUBLANES, t))
    for tl in range(t // LANES):
        lanes = slice(tl * LANES, (tl + 1) * LANES)
        w = None
        for h in range(heads):
            th = jnp.concatenate([bc_scr[0, h, :, lanes]] * (nk // SUBLANES), axis=0)
            e1 = jnp.concatenate([bc_scr[1, h, :, lanes]] * (nk // SUBLANES), axis=0)
            term = jnp.where(s2_ref[h, :, lanes] >= th, e2_ref[h, :, lanes], 0.0) * e1
            w = term if w is None else w + term
        a_t = a[:, lanes]
        act = 0.5 * a_t * (1.0 + lax.erf(a_t * (1.0 / math.sqrt(2.0))))
        g_out[row0:row0 + nk, lanes] = (w * act).astype(BF16)


def _peer_kernel(x_ref, g_ref, sh_ref, sc_ref, gt_ref, gfin_ref, th_ref, e1_ref, s2_ref, e2_ref,
                 u_ref, vtp_ref, vtc_ref, vtl_ref, o_ref, ht_scr, ga_scr, gb_scr, acc_scr, bc_scr,
                 *, final_norm):
    kb = pl.program_id(1)
    nk = th_ref.shape[1]
    n_sub = PEER_EB // PEER_SUB
    sel = (th_ref, e1_ref, s2_ref, e2_ref)

    @pl.when(kb == 0)
    def _():
        h = _rmsnorm_mod(x_ref[...], g_ref[...], sh_ref[...], sc_ref[...])
        ht_scr[...] = h.T.astype(BF16)
        gb_scr[...] = jnp.zeros(gb_scr.shape, BF16)
        acc_scr[...] = jnp.zeros(acc_scr.shape, F32)

    def first_matmul(idx):
        rows = slice(idx * PEER_SUB, (idx + 1) * PEER_SUB)
        return jnp.dot(u_ref[rows, :], ht_scr[...], preferred_element_type=F32)

    a_next = first_matmul(0)
    for idx in range(2 * n_sub):
        phase, sb = divmod(idx, n_sub)
        a = a_next
        if idx + 1 < 2 * n_sub:
            a_next = first_matmul(idx + 1)
        if sb == min(1, n_sub - 1):
            vt_ref, g_in = (vtp_ref, gb_scr) if phase == 0 else (vtc_ref, ga_scr)
            acc_scr[...] += jnp.dot(vt_ref[...], g_in[...], preferred_element_type=F32)
        g_out = ga_scr if phase == 0 else gb_scr
        for gi in range(PEER_SUB // nk):
            i = kb * (2 * PEER_EB // nk) + idx * (PEER_SUB // nk) + gi
            _peer_weights(i, a[gi * nk:(gi + 1) * nk, :], g_out, sb * PEER_SUB + gi * nk, bc_scr, *sel)

    @pl.when(kb == pl.num_programs(1) - 1)
    def _():
        acc = acc_scr[...] + jnp.dot(vtl_ref[...], gb_scr[...], preferred_element_type=F32)
        y = x_ref[...] + gt_ref[...] * acc.T
        if final_norm:
            y = y * lax.rsqrt(jnp.mean(y * y, axis=-1, keepdims=True) + EPS) * gfin_ref[...]
        o_ref[...] = y


def _peer_call(x2, g, sh, sc, gt, g_final, th, e1, s2, e2, u, vt, rows_per_mod, final_norm):
    m, d = x2.shape
    heads, nk, _ = th.shape
    e = u.shape[0]
    t = _tile(rows_per_mod, 512)
    per = rows_per_mod // t
    eb = PEER_EB
    nkb = e // (2 * eb)
    assert nkb * 2 * eb == e
    sel = pl.BlockSpec((heads, nk, t), lambda i, k: (0, 0, i))
    mod = pl.BlockSpec((None, 1, d), lambda i, k: (i // per, 0, 0))
    vec = pl.BlockSpec((1, d), lambda i, k: (0, 0))
    return pl.pallas_call(
        functools.partial(_peer_kernel, final_norm=final_norm),
        grid=(m // t, nkb),
        in_specs=[pl.BlockSpec((t, d), lambda i, k: (i, 0)),
                  vec, mod, mod, mod, vec, sel, sel, sel, sel,
                  pl.BlockSpec((2 * eb, d), lambda i, k: (k, 0)),
                  pl.BlockSpec((d, eb), lambda i, k: (0, jnp.maximum(2 * k - 1, 0))),
                  pl.BlockSpec((d, eb), lambda i, k: (0, 2 * k)),
                  pl.BlockSpec((d, eb), lambda i, k: (0, 2 * nkb - 1))],
        out_specs=pl.BlockSpec((t, d), lambda i, k: (i, 0)),
        out_shape=jax.ShapeDtypeStruct((m, d), F32),
        scratch_shapes=[pltpu.VMEM((d, t), BF16), pltpu.VMEM((eb, t), BF16),
                        pltpu.VMEM((eb, t), BF16), pltpu.VMEM((d, t), F32),
                        pltpu.VMEM((2, heads, SUBLANES, t), F32)],
        compiler_params=_params(("parallel", "arbitrary")),
        name="peer_experts",
    )(x2, g, sh, sc, gt, g_final, th, e1, s2, e2, u, vt, vt, vt)


def _peer_layer(x2, g_ffn, sh, sc, gt, g_final, w_q, keys, u, v, seq, final_norm):
    qk = _nmm_call(x2, g_ffn, sh, sc, w_q.astype(BF16), seq, BF16)
    kb = keys.reshape(2 * PEER_HEADS, PEER_NKEYS, PEER_DKEY // 2).astype(BF16)
    th, e1, s2, e2 = _topk_call(qk, kb)
    return _peer_call(x2, g_ffn, sh, sc, gt, g_final, th, e1, s2, e2,
                      u.astype(BF16), v.astype(BF16).T, seq, final_norm)


def kernel(x, c, ctx, c_ctx, w_ada, b_ada, g_mix, g_ffn, hg_w_in, hg_w_out, hg_gnorm, hg_lb,
           cv_w_pw1, cv_b_pw1, cv_w_dw, cv_b_dw, cv_ln_g, cv_ln_b, cv_w_pw2, cv_b_pw2,
           peer_w_q, peer_keys, peer_u, peer_v, g_final):
    b, seq, d = x.shape
    ctx_len = ctx.shape[1]
    depth = w_ada.shape[0]
    heads = d // LANES
    assert heads * LANES == d and seq % SCAN_CHUNK == 0 and ctx_len % SCAN_CHUNK == 0

    pad_rows = (-(b + 1)) % SUBLANES
    cc = jnp.concatenate([c, c_ctx[None, :], jnp.zeros((pad_rows, d), F32)], axis=0)
    ada = _ada_call(cc, w_ada, b_ada)

    x2 = x.reshape(b * seq, d)
    for l in range(depth):
        mixer = l % N_MIXERS
        mods = ada[l, :b].reshape(b, 1, 6, d)
        sh_m, sc_m, gt_m, sh_f, sc_f, gt_f = [mods[:, :, i, :] for i in range(6)]
        g_mix_l = g_mix[l][None, :]
        g_ffn_l = g_ffn[l][None, :]
        if mixer == 0:
            a = l // N_MIXERS
            w_in = hg_w_in[a].astype(BF16)
            cmods = ada[l, b].reshape(6, d)
            csh = jnp.broadcast_to(cmods[0][None, None, :], (b, 1, d))
            csc = jnp.broadcast_to(cmods[1][None, None, :], (b, 1, d))
            zc = _nmm_call(ctx.reshape(b * ctx_len, d), g_mix_l, csh, csc, w_in[:, :3 * d], ctx_len, F32)
            s0 = jnp.zeros((b, heads // 2, LANES, 2 * LANES), F32)
            s_f, s_b = _state_call(zc.reshape(b, ctx_len, 3 * d), hg_lb, s0, s0, l)
            z = _nmm_call(x2, g_mix_l, sh_m, sc_m, w_in, seq, F32)
            o_f, o_b, _, _ = _scan_call(z.reshape(b, seq, 5 * d), hg_lb, s_f, s_b, l)
            x2 = _mix_out_call(o_f.reshape(b * seq, d), o_b.reshape(b * seq, d), z,
                               hg_gnorm[a][None, :], hg_w_out[a].astype(BF16), x2, gt_m, seq)
        else:
            bi = l // N_MIXERS
            y = _nmm_call(x2, g_mix_l, sh_m, sc_m, cv_w_pw1[bi].astype(BF16), seq, F32,
                          glu_bias=cv_b_pw1[bi][None, :])
            y = _conv_call(y.reshape(b, seq, d), cv_w_dw[bi])
            x2 = _ln_out_call(y.reshape(b * seq, d), cv_b_dw[bi][None, :], cv_ln_g[bi][None, :],
                              cv_ln_b[bi][None, :], cv_w_pw2[bi].astype(BF16),
                              cv_b_pw2[bi][None, :], x2, gt_m, seq)
        x2 = _peer_layer(x2, g_ffn_l, sh_f, sc_f, gt_f, g_final[None, :], peer_w_q[l], peer_keys[l],
                         peer_u[l], peer_v[l], seq, final_norm=(l == depth - 1))
    return x2.reshape(b, seq, d)
```

```python
import functools
import math

import numpy as np
import jax
import jax.numpy as jnp
from jax import lax
from jax.experimental import pallas as pl
from jax.experimental.pallas import tpu as pltpu

EPS = 1e-6
GRID_W = 64
CONV_WIDTH = 31
HG_HEADS = 8
PEER_HEADS = 8
PEER_NKEYS = 128
PEER_DKEY = 256
PEER_TOPK = 16
N_MIXERS = 2

LANES = 128
SUBLANES = 8
SCAN_CHUNK = 128
SCAN_MATMUL_LEVELS = int(math.log2(SUBLANES))
PEER_EB = 512
PEER_SUB = 256
MXU_WIDTH = 256
VMEM_LIMIT = 56 * 1024 * 1024

F32 = jnp.float32
BF16 = jnp.bfloat16


def _tile(n, pref):
    t = min(n, pref)
    while n % t:
        t //= 2
    return t


def _params(sem):
    return pltpu.CompilerParams(dimension_semantics=sem, vmem_limit_bytes=VMEM_LIMIT)


def _rmsnorm_mod(x, g, sh, sc):
    y = x * lax.rsqrt(jnp.mean(x * x, axis=-1, keepdims=True) + EPS) * g
    return y * (1.0 + sc) + sh


def _ada_kernel(c_ref, w_ref, b_ref, o_ref):
    s = c_ref[...]
    s = s * jax.nn.sigmoid(s)
    o_ref[...] = jnp.dot(s, w_ref[...], preferred_element_type=F32,
                         precision=lax.Precision.HIGHEST) + b_ref[...]


def _ada_call(cc, w_ada, b_ada):
    depth, d, n = w_ada.shape
    r = cc.shape[0]
    tn = _tile(n, 1024)
    return pl.pallas_call(
        _ada_kernel,
        grid=(depth, n // tn),
        in_specs=[pl.BlockSpec((r, d), lambda l, j: (0, 0)),
                  pl.BlockSpec((None, d, tn), lambda l, j: (l, 0, j)),
                  pl.BlockSpec((None, 1, tn), lambda l, j: (l, 0, j))],
        out_specs=pl.BlockSpec((None, r, tn), lambda l, j: (l, 0, j)),
        out_shape=jax.ShapeDtypeStruct((depth, r, n), F32),
        compiler_params=_params(("parallel", "parallel")),
        name="ada",
    )(cc, w_ada, b_ada.reshape(depth, 1, n))


def _nmm_kernel(x_ref, g_ref, sh_ref, sc_ref, w_ref, o_ref, h_scr):
    @pl.when(pl.program_id(1) == 0)
    def _():
        h_scr[...] = _rmsnorm_mod(x_ref[...], g_ref[...], sh_ref[...], sc_ref[...]).astype(BF16)

    o_ref[...] = jnp.dot(h_scr[...], w_ref[...], preferred_element_type=F32).astype(o_ref.dtype)


def _nmm_glu_kernel(x_ref, g_ref, sh_ref, sc_ref, wa_ref, wg_ref, ba_ref, bg_ref, o_ref, h_scr):
    @pl.when(pl.program_id(1) == 0)
    def _():
        h_scr[...] = _rmsnorm_mod(x_ref[...], g_ref[...], sh_ref[...], sc_ref[...]).astype(BF16)

    h = h_scr[...]
    a = jnp.dot(h, wa_ref[...], preferred_element_type=F32) + ba_ref[...]
    g = jnp.dot(h, wg_ref[...], preferred_element_type=F32) + bg_ref[...]
    o_ref[...] = (a * jax.nn.sigmoid(g)).astype(o_ref.dtype)


def _nmm_call(x2, g, sh, sc, w, rows_per_mod, out_dtype, glu_bias=None):
    m, d = x2.shape
    n = w.shape[1]
    tm = _tile(rows_per_mod, 1024)
    per = rows_per_mod // tm
    x_spec = pl.BlockSpec((tm, d), lambda i, j: (i, 0))
    g_spec = pl.BlockSpec((1, d), lambda i, j: (0, 0))
    mod_spec = pl.BlockSpec((None, 1, d), lambda i, j: (i // per, 0, 0))
    scratch = [pltpu.VMEM((tm, d), BF16)]
    if glu_bias is None:
        tn = _tile(n, 1024)
        return pl.pallas_call(
            _nmm_kernel,
            grid=(m // tm, n // tn),
            in_specs=[x_spec, g_spec, mod_spec, mod_spec,
                      pl.BlockSpec((d, tn), lambda i, j: (0, j))],
            out_specs=pl.BlockSpec((tm, tn), lambda i, j: (i, j)),
            out_shape=jax.ShapeDtypeStruct((m, n), out_dtype),
            scratch_shapes=scratch,
            compiler_params=_params(("parallel", "arbitrary")),
            name="norm_mod_matmul",
        )(x2, g, sh, sc, w)
    nh = n // 2
    tn = _tile(nh, 1024)
    nb = nh // tn
    return pl.pallas_call(
        _nmm_glu_kernel,
        grid=(m // tm, nb),
        in_specs=[x_spec, g_spec, mod_spec, mod_spec,
                  pl.BlockSpec((d, tn), lambda i, j: (0, j)),
                  pl.BlockSpec((d, tn), lambda i, j: (0, j + nb)),
                  pl.BlockSpec((1, tn), lambda i, j: (0, j)),
                  pl.BlockSpec((1, tn), lambda i, j: (0, j + nb))],
        out_specs=pl.BlockSpec((tm, tn), lambda i, j: (i, j)),
        out_shape=jax.ShapeDtypeStruct((m, nh), out_dtype),
        scratch_shapes=scratch,
        compiler_params=_params(("parallel", "arbitrary")),
        name="norm_mod_matmul_glu",
    )(x2, g, sh, sc, w, w, glu_bias, glu_bias)


@functools.lru_cache(maxsize=None)
def _scan_consts(c):
    nl = int(math.log2(c))
    assert 1 << nl == c
    idx = np.arange(c)
    mats = np.zeros((2, 2 + SCAN_MATMUL_LEVELS, c, c), np.float32)
    level = np.zeros((2, c, c), np.int32)
    for d in range(2):
        tau = idx if d == 0 else c - 1 - idx
        tt, tr = tau[:, None], tau[None, :]
        mats[d, 0] = tr <= tt
        mats[d, 1] = tr > tt
        for l in range(SCAN_MATMUL_LEVELS):
            half = 1 << l
            mid = (tau // (2 * half)) * (2 * half) + half - 1
            later = ((tau >> l) & 1) == 1
            mats[d, 2 + l] = ((later[:, None] & (tr > mid[:, None]) & (tr <= tt))
                              | ((~later)[:, None] & (tr > tt) & (tr <= mid[:, None])))
        xor = tt ^ tr
        lev = np.floor(np.log2(np.maximum(xor, 1))).astype(np.int32)
        level[d] = np.where(tt > tr, lev, np.where(tt == tr, -1, -2))
    return mats.reshape(2, (2 + SCAN_MATMUL_LEVELS) * c, c), level, nl


def _scan_direction(d, zf, v, q, lb, est_ref, lev_ref, st_ref, o_ref, n_levels):
    c = zf.shape[0]
    f = lb + (1.0 - lb) * jax.nn.sigmoid(zf)
    logf = jnp.log(f)
    kk = (1.0 - lb) * jax.nn.sigmoid(-zf)
    lev = lev_ref[d]

    def block_diag(x):
        zero = jnp.zeros((x.shape[0], LANES), BF16)
        return jnp.concatenate([jnp.concatenate([x[:, :LANES], zero], axis=1),
                                jnp.concatenate([zero, x[:, LANES:]], axis=1)], axis=0)

    nt = (((1,), (1,)), ((), ()))
    for hp in range(st_ref.shape[0]):
        sl = slice(2 * hp * LANES, 2 * (hp + 1) * LANES)
        lf = logf[:, sl]
        hi = lf.astype(BF16)
        mid = (lf - hi.astype(F32)).astype(BF16)
        ex = jnp.dot(est_ref[d], jnp.concatenate([hi, mid], axis=0), preferred_element_type=F32)
        p = jnp.exp(ex)
        dec = jnp.exp(ex[0:1, :] + ex[c:c + 1, :])
        q_h, k_h, v_h = q[:, sl], kk[:, sl], v[:, sl]
        st = st_ref[hp]
        qb = (q_h * p[0:c]).astype(BF16)
        inter = lax.dot_general(qb, block_diag(st.astype(BF16)), nt, preferred_element_type=F32)
        pr = lax.dot_general(q_h.astype(BF16), block_diag(k_h.astype(BF16)), nt,
                             preferred_element_type=F32)
        scores = jnp.where(lev == -1, pr, 0.0)
        b = ex[0:c]
        for l in range(n_levels):
            if l < SCAN_MATMUL_LEVELS:
                pl_ = p[(2 + l) * c:(3 + l) * c]
            else:
                half = 1 << l
                rows = [jnp.broadcast_to(b[p0 + half - 1 + d:p0 + half + d, :], (2 * half, 2 * LANES))
                        for p0 in range(0, c, 2 * half)]
                bmid = jnp.concatenate(rows, axis=0) if len(rows) > 1 else rows[0]
                pl_ = jnp.exp(-jnp.abs(b - bmid))
            ql = (q_h * pl_).astype(BF16)
            kl = (k_h * pl_).astype(BF16)
            pr = lax.dot_general(ql, block_diag(kl), nt, preferred_element_type=F32)
            scores = jnp.where(lev == l, pr, scores)
        intra = jnp.dot(scores.astype(BF16), block_diag(v_h.astype(BF16)), preferred_element_type=F32)
        o_ref[:, sl] = (inter + intra).astype(o_ref.dtype)
        kend = (k_h * p[c:2 * c]).astype(BF16)
        vt = jnp.concatenate([v_h[:, :LANES].T, v_h[:, LANES:].T], axis=1).astype(BF16)
        st_ref[hp] = st * dec + jnp.dot(vt, block_diag(kend), preferred_element_type=F32)


def _scan_kernel(lbraw_ref, est_ref, lev_ref, s0f_ref, s0b_ref,
                 vf_ref, ff_ref, qf_ref, vb_ref, fb_ref, qb_ref,
                 of_ref, ob_ref, sf_ref, sb_ref, stf_scr, stb_scr, *, layer, n_levels):
    j = pl.program_id(1)

    @pl.when(j == 0)
    def _():
        stf_scr[...] = s0f_ref[...]
        stb_scr[...] = s0b_ref[...]

    raw = lbraw_ref[...]
    e = jnp.exp(raw - jnp.max(raw, axis=0, keepdims=True))
    lb_all = jnp.sum(e[0:layer + 1], axis=0, keepdims=True) / jnp.sum(e, axis=0, keepdims=True)
    d_model = vf_ref.shape[-1]
    _scan_direction(0, ff_ref[...], vf_ref[...], qf_ref[...], lb_all[:, :d_model],
                    est_ref, lev_ref, stf_scr, of_ref, n_levels)
    _scan_direction(1, fb_ref[...], vb_ref[...], qb_ref[...], lb_all[:, d_model:],
                    est_ref, lev_ref, stb_scr, ob_ref, n_levels)

    @pl.when(j == pl.num_programs(1) - 1)
    def _():
        sf_ref[...] = stf_scr[...]
        sb_ref[...] = stb_scr[...]


def _state_direction(d, zf, v, lb, est_ref, st_ref):
    c = zf.shape[0]
    f = lb + (1.0 - lb) * jax.nn.sigmoid(zf)
    logf = jnp.log(f)
    kk = (1.0 - lb) * jax.nn.sigmoid(-zf)
    for hp in range(st_ref.shape[0]):
        sl = slice(2 * hp * LANES, 2 * (hp + 1) * LANES)
        lf = logf[:, sl]
        hi = lf.astype(BF16)
        mid = (lf - hi.astype(F32)).astype(BF16)
        ex = jnp.dot(est_ref[d, 0:2 * c, :], jnp.concatenate([hi, mid], axis=0),
                     preferred_element_type=F32)
        dec = jnp.exp(ex[0:1, :] + ex[c:c + 1, :])
        k_h, v_h = kk[:, sl], v[:, sl]
        kend = (k_h * jnp.exp(ex[c:2 * c])).astype(BF16)
        zero = jnp.zeros((c, LANES), BF16)
        kend_bd = jnp.concatenate([jnp.concatenate([kend[:, :LANES], zero], axis=1),
                                   jnp.concatenate([zero, kend[:, LANES:]], axis=1)], axis=0)
        vt = jnp.concatenate([v_h[:, :LANES].T, v_h[:, LANES:].T], axis=1).astype(BF16)
        st_ref[hp] = st_ref[hp] * dec + jnp.dot(vt, kend_bd, preferred_element_type=F32)


def _state_kernel(lbraw_ref, est_ref, s0f_ref, s0b_ref, vf_ref, ff_ref, vb_ref, fb_ref,
                  sf_ref, sb_ref, stf_scr, stb_scr, *, layer):
    j = pl.program_id(1)

    @pl.when(j == 0)
    def _():
        stf_scr[...] = s0f_ref[...]
        stb_scr[...] = s0b_ref[...]

    raw = lbraw_ref[...]
    e = jnp.exp(raw - jnp.max(raw, axis=0, keepdims=True))
    lb_all = jnp.sum(e[0:layer + 1], axis=0, keepdims=True) / jnp.sum(e, axis=0, keepdims=True)
    d_model = vf_ref.shape[-1]
    _state_direction(0, ff_ref[...], vf_ref[...], lb_all[:, :d_model], est_ref, stf_scr)
    _state_direction(1, fb_ref[...], vb_ref[...], lb_all[:, d_model:], est_ref, stb_scr)

    @pl.when(j == pl.num_programs(1) - 1)
    def _():
        sf_ref[...] = stf_scr[...]
        sb_ref[...] = stb_scr[...]


def _state_call(z, hg_lb, s0f, s0b, layer):
    b, l, d3 = z.shape
    d = d3 // 3
    pairs = d // (2 * LANES)
    c = SCAN_CHUNK
    nc = l // c
    est, _, _ = _scan_consts(c)
    est = jnp.asarray(np.concatenate([est, est], axis=2), BF16)

    def zspec(col, rev):
        if rev:
            return pl.BlockSpec((None, c, d), lambda bi, j: (bi, nc - 1 - j, col))
        return pl.BlockSpec((None, c, d), lambda bi, j: (bi, j, col))

    full = lambda a: pl.BlockSpec(a.shape, lambda bi, j: (0,) * a.ndim)
    st_spec = pl.BlockSpec((None, pairs, LANES, 2 * LANES), lambda bi, j: (bi, 0, 0, 0))
    return pl.pallas_call(
        functools.partial(_state_kernel, layer=layer),
        grid=(b, nc),
        in_specs=[full(hg_lb), full(est), st_spec, st_spec,
                  zspec(0, False), zspec(1, False), zspec(0, True), zspec(2, True)],
        out_specs=[st_spec, st_spec],
        out_shape=[jax.ShapeDtypeStruct(s0f.shape, F32), jax.ShapeDtypeStruct(s0b.shape, F32)],
        scratch_shapes=[pltpu.VMEM((pairs, LANES, 2 * LANES), F32),
                        pltpu.VMEM((pairs, LANES, 2 * LANES), F32)],
        compiler_params=_params(("parallel", "arbitrary")),
        name="hgrn2_state",
    )(hg_lb, est, s0f, s0b, z, z, z, z)


def _scan_call(z, hg_lb, s0f, s0b, layer):
    b, l, d5 = z.shape
    d = d5 // 5
    pairs = d // (2 * LANES)
    c = SCAN_CHUNK
    nc = l // c
    est, lev, n_levels = _scan_consts(c)
    est = jnp.asarray(np.concatenate([est, est], axis=2), BF16)
    lev = jnp.asarray(np.concatenate([lev, lev], axis=2))

    def zspec(col, rev):
        if rev:
            return pl.BlockSpec((None, c, d), lambda bi, j: (bi, nc - 1 - j, col))
        return pl.BlockSpec((None, c, d), lambda bi, j: (bi, j, col))

    full = lambda a: pl.BlockSpec(a.shape, lambda bi, j: (0,) * a.ndim)
    st_spec = pl.BlockSpec((None, pairs, LANES, 2 * LANES), lambda bi, j: (bi, 0, 0, 0))
    return pl.pallas_call(
        functools.partial(_scan_kernel, layer=layer, n_levels=n_levels),
        grid=(b, nc),
        in_specs=[full(hg_lb), full(est), full(lev), st_spec, st_spec,
                  zspec(0, False), zspec(1, False), zspec(3, False),
                  zspec(0, True), zspec(2, True), zspec(3, True)],
        out_specs=[pl.BlockSpec((None, c, d), lambda bi, j: (bi, j, 0)),
                   pl.BlockSpec((None, c, d), lambda bi, j: (bi, nc - 1 - j, 0)),
                   st_spec, st_spec],
        out_shape=[jax.ShapeDtypeStruct((b, l, d), BF16), jax.ShapeDtypeStruct((b, l, d), BF16),
                   jax.ShapeDtypeStruct(s0f.shape, F32), jax.ShapeDtypeStruct(s0b.shape, F32)],
        scratch_shapes=[pltpu.VMEM((pairs, LANES, 2 * LANES), F32),
                        pltpu.VMEM((pairs, LANES, 2 * LANES), F32)],
        compiler_params=_params(("parallel", "arbitrary")),
        name="hgrn2_scan",
    )(hg_lb, est, lev, s0f, s0b, z, z, z, z, z, z)


def _mix_out_kernel(of_ref, ob_ref, gate_ref, gn_ref, w_ref, x_ref, gt_ref, o_ref, h_scr):
    @pl.when(pl.program_id(1) == 0)
    def _():
        o = of_ref[...].astype(F32) + ob_ref[...].astype(F32)
        gate = gate_ref[...]
        gn = gn_ref[...]
        for h in range(o.shape[1] // LANES):
            sl = slice(h * LANES, (h + 1) * LANES)
            oh = o[:, sl]
            y = oh * lax.rsqrt(jnp.mean(oh * oh, axis=-1, keepdims=True) + EPS) * gn
            gh = gate[:, sl]
            h_scr[:, sl] = (y * (gh * jax.nn.sigmoid(gh))).astype(BF16)

    mix = jnp.dot(h_scr[...], w_ref[...], preferred_element_type=F32)
    o_ref[...] = x_ref[...] + gt_ref[...] * mix


def _mix_out_call(o_f, o_b, z2, gnorm, w_out, x2, gt, rows_per_mod):
    m, d = x2.shape
    tm = _tile(rows_per_mod, 1024)
    per = rows_per_mod // tm
    tn = _tile(d, 1024)
    row = pl.BlockSpec((tm, d), lambda i, j: (i, 0))
    return pl.pallas_call(
        _mix_out_kernel,
        grid=(m // tm, d // tn),
        in_specs=[row, row,
                  pl.BlockSpec((tm, d), lambda i, j: (i, 4)),
                  pl.BlockSpec((1, LANES), lambda i, j: (0, 0)),
                  pl.BlockSpec((d, tn), lambda i, j: (0, j)),
                  pl.BlockSpec((tm, tn), lambda i, j: (i, j)),
                  pl.BlockSpec((None, 1, tn), lambda i, j: (i // per, 0, j))],
        out_specs=pl.BlockSpec((tm, tn), lambda i, j: (i, j)),
        out_shape=jax.ShapeDtypeStruct((m, d), F32),
        scratch_shapes=[pltpu.VMEM((tm, d), BF16)],
        compiler_params=_params(("parallel", "arbitrary")),
        name="hgrn2_out",
    )(o_f, o_b, z2, gnorm, w_out, x2, gt)


def _conv_kernel(y_ref, w_ref, o_ref, padw_scr, padh_scr, *, rows, n_w_tiles):
    ct = pl.program_id(1)
    half = (CONV_WIDTH - 1) // 2
    l, lanes = y_ref.shape
    lead = padw_scr.shape[1] - GRID_W - 16

    @pl.when(ct < n_w_tiles)
    def _():
        padw_scr[...] = jnp.zeros(padw_scr.shape, F32)
        padw_scr[:, lead:lead + GRID_W, :] = y_ref[...].reshape(rows, GRID_W, lanes)
        acc = jnp.zeros((rows, GRID_W, lanes), F32)
        for k in range(CONV_WIDTH):
            start = lead + k - half
            acc = acc + w_ref[k:k + 1, :].reshape(1, 1, lanes) * padw_scr[:, start:start + GRID_W, :]
        o_ref[...] = acc.reshape(l, lanes)

    @pl.when(ct >= n_w_tiles)
    def _():
        margin = half * GRID_W
        padh_scr[...] = jnp.zeros(padh_scr.shape, F32)
        padh_scr[margin:margin + l, :] = y_ref[...]
        acc = jnp.zeros((l, lanes), F32)
        for k in range(CONV_WIDTH):
            acc = acc + w_ref[k:k + 1, :] * padh_scr[k * GRID_W:k * GRID_W + l, :]
        o_ref[...] = acc


def _conv_call(y, w_dw):
    b, l, c = y.shape
    rows = l // GRID_W
    half = (CONV_WIDTH - 1) // 2
    n_w_tiles = (c // 2) // LANES
    wp = jnp.pad(w_dw, ((0, 32 - CONV_WIDTH), (0, 0)))
    return pl.pallas_call(
        functools.partial(_conv_kernel, rows=rows, n_w_tiles=n_w_tiles),
        grid=(b, c // LANES),
        in_specs=[pl.BlockSpec((None, l, LANES), lambda bi, ct: (bi, 0, ct)),
                  pl.BlockSpec((32, LANES), lambda bi, ct: (0, ct))],
        out_specs=pl.BlockSpec((None, l, LANES), lambda bi, ct: (bi, 0, ct)),
        out_shape=jax.ShapeDtypeStruct((b, l, c), F32),
        scratch_shapes=[pltpu.VMEM((rows, GRID_W + 32, LANES), F32),
                        pltpu.VMEM((l + 2 * half * GRID_W, LANES), F32)],
        compiler_params=_params(("parallel", "parallel")),
        name="axial_dwconv",
    )(y, wp)


def _ln_out_kernel(y_ref, bdw_ref, lg_ref, lb_ref, w_ref, b2_ref, x_ref, gt_ref, o_ref, h_scr):
    @pl.when(pl.program_id(1) == 0)
    def _():
        y = y_ref[...] + bdw_ref[...]
        yc = y - jnp.mean(y, axis=-1, keepdims=True)
        yn = yc * lax.rsqrt(jnp.mean(yc * yc, axis=-1, keepdims=True) + EPS)
        yn = yn * lg_ref[...] + lb_ref[...]
        h_scr[...] = (yn * jax.nn.sigmoid(yn)).astype(BF16)

    mix = jnp.dot(h_scr[...], w_ref[...], preferred_element_type=F32) + b2_ref[...]
    o_ref[...] = x_ref[...] + gt_ref[...] * mix


def _ln_out_call(y2, b_dw, ln_g, ln_b, w_pw2, b_pw2, x2, gt, rows_per_mod):
    m, d = x2.shape
    tm = _tile(rows_per_mod, 1024)
    per = rows_per_mod // tm
    tn = _tile(d, 1024)
    vec = pl.BlockSpec((1, d), lambda i, j: (0, 0))
    return pl.pallas_call(
        _ln_out_kernel,
        grid=(m // tm, d // tn),
        in_specs=[pl.BlockSpec((tm, d), lambda i, j: (i, 0)), vec, vec, vec,
                  pl.BlockSpec((d, tn), lambda i, j: (0, j)),
                  pl.BlockSpec((1, tn), lambda i, j: (0, j)),
                  pl.BlockSpec((tm, tn), lambda i, j: (i, j)),
                  pl.BlockSpec((None, 1, tn), lambda i, j: (i // per, 0, j))],
        out_specs=pl.BlockSpec((tm, tn), lambda i, j: (i, j)),
        out_shape=jax.ShapeDtypeStruct((m, d), F32),
        scratch_shapes=[pltpu.VMEM((tm, d), BF16)],
        compiler_params=_params(("parallel", "arbitrary")),
        name="conv_out",
    )(y2, b_dw, ln_g, ln_b, w_pw2, b_pw2, x2, gt)


def _vmax(a, b):
    if a is None:
        return b
    if b is None:
        return a
    return jnp.maximum(a, b)


def _vmin(a, b):
    if a is None or b is None:
        return None
    return jnp.minimum(a, b)


def _bitonic_merge_desc(xs):
    n = len(xs)
    j = n // 2
    while j >= 1:
        for i in range(n):
            o = i ^ j
            if o > i:
                a, b = xs[i], xs[o]
                xs[i], xs[o] = _vmax(a, b), _vmin(a, b)
        j //= 2
    return xs


def _bitonic_sort_desc(xs):
    n = len(xs)
    k = 2
    while k <= n:
        j = k // 2
        while j >= 1:
            for i in range(n):
                o = i ^ j
                if o > i:
                    a, b = xs[i], xs[o]
                    if (i & k) == 0:
                        xs[i], xs[o] = _vmax(a, b), _vmin(a, b)
                    else:
                        xs[i], xs[o] = _vmin(a, b), _vmax(a, b)
            j //= 2
        k *= 2
    return xs


def _merge_top(a, b):
    n = len(a)
    b = list(b) + [None] * (n - len(b))
    return _bitonic_merge_desc([_vmax(a[i], b[n - 1 - i]) for i in range(n)])


def _topk_kernel(q_ref, keys_ref, th_ref, e1_ref, s2_ref, e2_ref, s_scr):
    heads = th_ref.shape[0]
    nk = keys_ref.shape[1]
    k = PEER_TOPK
    t = q_ref.shape[0]
    sub = lax.broadcasted_iota(jnp.int32, (SUBLANES, t), 0)
    tops = [[None] * k, [None] * k]
    for h in range(heads):
        for p in range(2):
            hp = 2 * h + p
            dk = keys_ref.shape[2]
            s = lax.dot_general(keys_ref[hp], q_ref[:, hp * dk:(hp + 1) * dk],
                                (((1,), (1,)), ((), ())), preferred_element_type=F32)
            s_scr[hp] = s
            groups = _bitonic_sort_desc([s[SUBLANES * a:SUBLANES * (a + 1), :] for a in range(nk // SUBLANES)])
            assert len(groups) == k
            for shift in (4, 2, 1):
                groups = _merge_top(groups, [pltpu.roll(g, shift, 0) for g in groups])
            for a in range(k):
                prev = tops[p][a]
                tops[p][a] = groups[a] if prev is None else jnp.where(sub == h, groups[a], prev)
    v1, v2 = tops
    best = [v1[0] + v2[b] for b in range(k)]
    for a in range(1, k // 2):
        best = _merge_top(best, [v1[a] + v2[b] for b in range(k // (a + 1))])
    best = _merge_top(best, [v1[a] + v2[0] for a in range(k // 2, k)])
    cmax = best[0]
    zsum = jnp.ones_like(cmax)
    for a in range(1, k):
        zsum = zsum + jnp.exp(best[a] - cmax)
    tau = best[k - 1]
    inv_z = 1.0 / zsum
    for h in range(heads):
        s1 = s_scr[2 * h]
        s2 = s_scr[2 * h + 1]
        tau_h = tau[h:h + 1, :]
        slack = (jnp.abs(tau_h) + jnp.abs(s1)) * (2.0 ** -22)
        th_ref[h] = (tau_h - s1) - slack
        e1_ref[h] = jnp.exp(s1 - v1[0][h:h + 1, :])
        s2_ref[h] = s2
        e2_ref[h] = jnp.exp(s2 - v2[0][h:h + 1, :]) * inv_z[h:h + 1, :]


def _topk_call(qk, keys):
    m = qk.shape[0]
    hp, nk, dk = keys.shape
    heads = hp // 2
    t = _tile(m, 512)
    out = jax.ShapeDtypeStruct((heads, nk, m), F32)
    ospec = pl.BlockSpec((heads, nk, t), lambda i: (0, 0, i))
    return pl.pallas_call(
        _topk_kernel,
        grid=(m // t,),
        in_specs=[pl.BlockSpec((t, hp * dk), lambda i: (i, 0)),
                  pl.BlockSpec((hp, nk, dk), lambda i: (0, 0, 0))],
        out_specs=[ospec, ospec, ospec, ospec],
        out_shape=[out, out, out, out],
        scratch_shapes=[pltpu.VMEM((hp, nk, t), F32)],
        compiler_params=_params(("parallel",)),
        name="peer_topk",
    )(qk, keys)


def _peer_weights(i, a, g_out, row0, bc_scr, th_ref, e1_ref, s2_ref, e2_ref):
    heads, nk, t = th_ref.shape
    for h in range(heads):
        bc_scr[0, h] = jnp.broadcast_to(th_ref[h, pl.ds(i, 1), :], (SUBLANES, t))
        bc_scr[1, h] = jnp.broadcast_to(e1_ref[h, pl.ds(i, 1), :], (SUBLANES, t))
    for tl in range(t // LANES):
        lanes = slice(tl * LANES, (tl + 1) * LANES)
        w = None
        for h in range(heads):
            th = jnp.concatenate([bc_scr[0, h, :, lanes]] * (nk // SUBLANES), axis=0)
            e1 = jnp.concatenate([bc_scr[1, h, :, lanes]] * (nk // SUBLANES), axis=0)
            term = jnp.where(s2_ref[h, :, lanes] >= th, e2_ref[h, :, lanes], 0.0) * e1
            w = term if w is None else w + term
        a_t = a[:, lanes]
        act = 0.5 * a_t * (1.0 + lax.erf(a_t * (1.0 / math.sqrt(2.0))))
        g_out[row0:row0 + nk, lanes] = (w * act).astype(BF16)


def _peer_kernel(x_ref, g_ref, sh_ref, sc_ref, gt_ref, gfin_ref, th_ref, e1_ref, s2_ref, e2_ref,
                 u_ref, vtp_ref, vtc_ref, vtl_ref, o_ref, ht_scr, ga_scr, gb_scr, acc_scr, bc_scr,
                 *, final_norm):
    kb = pl.program_id(1)
    nk = th_ref.shape[1]
    n_sub = PEER_EB // PEER_SUB
    sel = (th_ref, e1_ref, s2_ref, e2_ref)

    @pl.when(kb == 0)
    def _():
        h = _rmsnorm_mod(x_ref[...], g_ref[...], sh_ref[...], sc_ref[...])
        ht_scr[...] = h.T.astype(BF16)
        gb_scr[...] = jnp.zeros(gb_scr.shape, BF16)
        acc_scr[...] = jnp.zeros(acc_scr.shape, F32)

    def first_matmul(idx):
        rows = slice(idx * PEER_SUB, (idx + 1) * PEER_SUB)
        return jnp.dot(u_ref[rows, :], ht_scr[...], preferred_element_type=F32)

    a_next = first_matmul(0)
    for idx in range(2 * n_sub):
        phase, sb = divmod(idx, n_sub)
        a = a_next
        if idx + 1 < 2 * n_sub:
            a_next = first_matmul(idx + 1)
        if sb == min(1, n_sub - 1):
            vt_ref, g_in = (vtp_ref, gb_scr) if phase == 0 else (vtc_ref, ga_scr)
            acc_scr[...] += jnp.dot(vt_ref[...], g_in[...], preferred_element_type=F32)
        g_out = ga_scr if phase == 0 else gb_scr
        for gi in range(PEER_SUB // nk):
            i = kb * (2 * PEER_EB // nk) + idx * (PEER_SUB // nk) + gi
            _peer_weights(i, a[gi * nk:(gi + 1) * nk, :], g_out, sb * PEER_SUB + gi * nk, bc_scr, *sel)

    @pl.when(kb == pl.num_programs(1) - 1)
    def _():
        acc = acc_scr[...] + jnp.dot(vtl_ref[...], gb_scr[...], preferred_element_type=F32)
        y = x_ref[...] + gt_ref[...] * acc.T
        if final_norm:
            y = y * lax.rsqrt(jnp.mean(y * y, axis=-1, keepdims=True) + EPS) * gfin_ref[...]
        o_ref[...] = y


def _peer_call(x2, g, sh, sc, gt, g_final, th, e1, s2, e2, u, vt, rows_per_mod, final_norm):
    m, d = x2.shape
    heads, nk, _ = th.shape
    e = u.shape[0]
    t = _tile(rows_per_mod, 512)
    per = rows_per_mod // t
    eb = PEER_EB
    nkb = e // (2 * eb)
    assert nkb * 2 * eb == e
    sel = pl.BlockSpec((heads, nk, t), lambda i, k: (0, 0, i))
    mod = pl.BlockSpec((None, 1, d), lambda i, k: (i // per, 0, 0))
    vec = pl.BlockSpec((1, d), lambda i, k: (0, 0))
    return pl.pallas_call(
        functools.partial(_peer_kernel, final_norm=final_norm),
        grid=(m // t, nkb),
        in_specs=[pl.BlockSpec((t, d), lambda i, k: (i, 0)),
                  vec, mod, mod, mod, vec, sel, sel, sel, sel,
                  pl.BlockSpec((2 * eb, d), lambda i, k: (k, 0)),
                  pl.BlockSpec((d, eb), lambda i, k: (0, jnp.maximum(2 * k - 1, 0))),
                  pl.BlockSpec((d, eb), lambda i, k: (0, 2 * k)),
                  pl.BlockSpec((d, eb), lambda i, k: (0, 2 * nkb - 1))],
        out_specs=pl.BlockSpec((t, d), lambda i, k: (i, 0)),
        out_shape=jax.ShapeDtypeStruct((m, d), F32),
        scratch_shapes=[pltpu.VMEM((d, t), BF16), pltpu.VMEM((eb, t), BF16),
                        pltpu.VMEM((eb, t), BF16), pltpu.VMEM((d, t), F32),
                        pltpu.VMEM((2, heads, SUBLANES, t), F32)],
        compiler_params=_params(("parallel", "arbitrary")),
        name="peer_experts",
    )(x2, g, sh, sc, gt, g_final, th, e1, s2, e2, u, vt, vt, vt)


def _peer_layer(x2, g_ffn, sh, sc, gt, g_final, w_q, keys, u, v, seq, final_norm):
    qk = _nmm_call(x2, g_ffn, sh, sc, w_q.astype(BF16), seq, BF16)
    kb = keys.reshape(2 * PEER_HEADS, PEER_NKEYS, PEER_DKEY // 2).astype(BF16)
    th, e1, s2, e2 = _topk_call(qk, kb)
    return _peer_call(x2, g_ffn, sh, sc, gt, g_final, th, e1, s2, e2,
                      u.astype(BF16), v.astype(BF16).T, seq, final_norm)


def kernel(x, c, ctx, c_ctx, w_ada, b_ada, g_mix, g_ffn, hg_w_in, hg_w_out, hg_gnorm, hg_lb,
           cv_w_pw1, cv_b_pw1, cv_w_dw, cv_b_dw, cv_ln_g, cv_ln_b, cv_w_pw2, cv_b_pw2,
           peer_w_q, peer_keys, peer_u, peer_v, g_final):
    b, seq, d = x.shape
    ctx_len = ctx.shape[1]
    depth = w_ada.shape[0]
    heads = d // LANES
    assert heads * LANES == d and seq % SCAN_CHUNK == 0 and ctx_len % SCAN_CHUNK == 0

    pad_rows = (-(b + 1)) % SUBLANES
    cc = jnp.concatenate([c, c_ctx[None, :], jnp.zeros((pad_rows, d), F32)], axis=0)
    ada = _ada_call(cc, w_ada, b_ada)

    x2 = x.reshape(b * seq, d)
    for l in range(depth):
        mixer = l % N_MIXERS
        mods = ada[l, :b].reshape(b, 1, 6, d)
        sh_m, sc_m, gt_m, sh_f, sc_f, gt_f = [mods[:, :, i, :] for i in range(6)]
        g_mix_l = g_mix[l][None, :]
        g_ffn_l = g_ffn[l][None, :]
        if mixer == 0:
            a = l // N_MIXERS
            w_in = hg_w_in[a].astype(BF16)
            cmods = ada[l, b].reshape(6, d)
            csh = jnp.broadcast_to(cmods[0][None, None, :], (b, 1, d))
            csc = jnp.broadcast_to(cmods[1][None, None, :], (b, 1, d))
            zc = _nmm_call(ctx.reshape(b * ctx_len, d), g_mix_l, csh, csc, w_in[:, :3 * d], ctx_len, F32)
            s0 = jnp.zeros((b, heads // 2, LANES, 2 * LANES), F32)
            s_f, s_b = _state_call(zc.reshape(b, ctx_len, 3 * d), hg_lb, s0, s0, l)
            z = _nmm_call(x2, g_mix_l, sh_m, sc_m, w_in, seq, F32)
            o_f, o_b, _, _ = _scan_call(z.reshape(b, seq, 5 * d), hg_lb, s_f, s_b, l)
            x2 = _mix_out_call(o_f.reshape(b * seq, d), o_b.reshape(b * seq, d), z,
                               hg_gnorm[a][None, :], hg_w_out[a].astype(BF16), x2, gt_m, seq)
        else:
            bi = l // N_MIXERS
            y = _nmm_call(x2, g_mix_l, sh_m, sc_m, cv_w_pw1[bi].astype(BF16), seq, F32,
                          glu_bias=cv_b_pw1[bi][None, :])
            y = _conv_call(y.reshape(b, seq, d), cv_w_dw[bi])
            x2 = _ln_out_call(y.reshape(b * seq, d), cv_b_dw[bi][None, :], cv_ln_g[bi][None, :],
                              cv_ln_b[bi][None, :], cv_w_pw2[bi].astype(BF16),
                              cv_b_pw2[bi][None, :], x2, gt_m, seq)
        x2 = _peer_layer(x2, g_ffn_l, sh_f, sc_f, gt_f, g_final[None, :], peer_w_q[l], peer_keys[l],
                         peer_u[l], peer_v[l], seq, final_norm=(l == depth - 1))
    return x2.reshape(b, seq, d)
```
